```python
import math
import jax, jax.numpy as jnp
from jax import lax
import numpy as np

D_MODEL = 1024
BATCH = 16
SEQ = 4096
DEPTH = 4

CHUNK = 64
D_PLE = 256
W_BRANCH = D_MODEL // 2
N_BRANCH_COLS = 6 * W_BRANCH
CONV_A_WIDTH = 31
GMLP_BLOCK = 2 * CHUNK
N_HEADS_B = 8
HEAD_DIM_B = W_BRANCH // N_HEADS_B
POOL_WINDOWS = (2, 4, 8, 16)
N_GROUPS_C = len(POOL_WINDOWS)
GROUP_DIM_C = W_BRANCH // N_GROUPS_C
CONV_D_WIDTH = 3
DEEPNORM_ALPHA = (2.0 * DEPTH) ** 0.25
DEEPNORM_BETA = (8.0 * DEPTH) ** -0.25
LN_EPS = 1e-5

kernel_name = 'hybrid_conv_gmlp_pool_shortconv_deepnorm'


def layer_norm(x, g, b):
    xf = x.astype(jnp.float32)
    mu = jnp.mean(xf, axis=-1, keepdims=True)
    var = jnp.mean(jnp.square(xf - mu), axis=-1, keepdims=True)
    y = (xf - mu) * lax.rsqrt(var + LN_EPS)
    return (y * g.astype(jnp.float32) + b.astype(jnp.float32)).astype(x.dtype)


def causal_depthwise_conv(x, w):
    k, c = w.shape
    return lax.conv_general_dilated(
        x, w[:, None, :], window_strides=(1,), padding=[(k - 1, 0)],
        dimension_numbers=('NWC', 'WIO', 'NWC'), feature_group_count=c)


def even_mixer(x, w_in, b_in, conv_w, conv_b, ln_a_g, ln_a_b, ln_v_g, ln_v_b, w_s, b_s, w_out, b_out):
    bt, s, _ = x.shape
    z = jnp.einsum('bsd,de->bse', x, w_in) + b_in
    a_val, a_glu, a_gate, u, v, g_gate = jnp.split(z, 6, axis=-1)
    a = a_val * jax.nn.sigmoid(a_glu)
    a = causal_depthwise_conv(a, conv_w) + conv_b
    a = jax.nn.silu(layer_norm(a, ln_a_g, ln_a_b)) * jax.nn.silu(a_gate)
    u = jax.nn.gelu(u)
    v = layer_norm(jax.nn.gelu(v), ln_v_g, ln_v_b)
    v = v.reshape(bt, s // GMLP_BLOCK, GMLP_BLOCK, N_HEADS_B, HEAD_DIM_B)
    mask = jnp.tril(jnp.ones((GMLP_BLOCK, GMLP_BLOCK), dtype=bool))
    w_s = jnp.where(mask[None], w_s, jnp.zeros_like(w_s))
    sg = jnp.einsum('hts,bnshd->bnthd', w_s, v) + b_s.T[:, :, None]
    g = u * sg.reshape(bt, s, W_BRANCH) * jax.nn.silu(g_gate)
    y = jnp.concatenate([a, g], axis=-1)
    return jnp.einsum('bse,ed->bsd', y, w_out) + b_out


def odd_mixer(x, w_in, b_in, w_pool, pool_scale, conv_w, w_out, b_out):
    bt, s, _ = x.shape
    z = jnp.einsum('bsd,de->bse', x, w_in) + b_in
    c_val, c_gate, d_h, d_b, d_c, d_gate = jnp.split(z, 6, axis=-1)
    vg = c_val.reshape(bt, s, N_GROUPS_C, GROUP_DIM_C)
    cs = jnp.cumsum(vg.astype(jnp.float32), axis=1)
    pos = jnp.arange(1, s + 1, dtype=jnp.float32)
    means = []
    for gi, win in enumerate(POOL_WINDOWS):
        c_g = cs[:, :, gi]
        lag = jnp.pad(c_g, ((0, 0), (win, 0), (0, 0)))[:, :s]
        means.append((c_g - lag) / jnp.minimum(pos, float(win))[:, None])
    pooled = jnp.stack(means, axis=2).astype(vg.dtype) - vg
    c = jnp.einsum('bsgc,gce->bsge', pooled, w_pool).reshape(bt, s, W_BRANCH) * pool_scale
    c = c * jax.nn.silu(c_gate)
    d = d_b * causal_depthwise_conv(d_c * d_h, conv_w)
    d = d * jax.nn.silu(d_gate)
    y = jnp.concatenate([c, d], axis=-1)
    return jnp.einsum('bse,ed->bsd', y, w_out) + b_out


def setup_inputs(seed: int = 0) -> dict:
    key = jax.random.key(seed)
    ks = jax.random.split(key, 26)
    ne = (DEPTH + 1) // 2
    no = DEPTH // 2
    f32 = jnp.float32
    nrm = lambda k, shape, scale: jax.random.normal(k, shape, f32) * scale
    return {
        'x': nrm(ks[0], (BATCH, SEQ, D_MODEL), 1.0),
        'p': nrm(ks[1], (DEPTH, BATCH, SEQ, D_PLE), 1.0),
        'w_in_e': nrm(ks[2], (ne, D_MODEL, N_BRANCH_COLS), D_MODEL ** -0.5),
        'b_in_e': nrm(ks[3], (ne, N_BRANCH_COLS), 0.02),
        'conv_a_w': nrm(ks[4], (ne, CONV_A_WIDTH, W_BRANCH), CONV_A_WIDTH ** -0.5),
        'conv_a_b': nrm(ks[5], (ne, W_BRANCH), 0.02),
        'ln_a_g': 1.0 + nrm(ks[6], (ne, W_BRANCH), 0.05),
        'ln_a_b': nrm(ks[7], (ne, W_BRANCH), 0.02),
        'ln_v_g': 1.0 + nrm(ks[8], (ne, W_BRANCH), 0.05),
        'ln_v_b': nrm(ks[9], (ne, W_BRANCH), 0.02),
        'w_s': nrm(ks[10], (ne, N_HEADS_B, GMLP_BLOCK, GMLP_BLOCK), 0.5 * GMLP_BLOCK ** -0.5),
        'b_s': 1.0 + nrm(ks[11], (ne, N_HEADS_B, GMLP_BLOCK), 0.1),
        'w_out_e': nrm(ks[12], (ne, 2 * W_BRANCH, D_MODEL), DEEPNORM_BETA * (2 * W_BRANCH) ** -0.5),
        'b_out_e': nrm(ks[13], (ne, D_MODEL), 0.02),
        'w_in_o': nrm(ks[14], (no, D_MODEL, N_BRANCH_COLS), D_MODEL ** -0.5),
        'b_in_o': nrm(ks[15], (no, N_BRANCH_COLS), 0.02),
        'w_pool': nrm(ks[16], (no, N_GROUPS_C, GROUP_DIM_C, GROUP_DIM_C), GROUP_DIM_C ** -0.5),
        'pool_scale': 1.0 + nrm(ks[17], (no, W_BRANCH), 0.1),
        'conv_d_w': nrm(ks[18], (no, CONV_D_WIDTH, W_BRANCH), CONV_D_WIDTH ** -0.5),
        'w_out_o': nrm(ks[19], (no, 2 * W_BRANCH, D_MODEL), DEEPNORM_BETA * (2 * W_BRANCH) ** -0.5),
        'b_out_o': nrm(ks[20], (no, D_MODEL), 0.02),
        'ln_g': 1.0 + nrm(ks[21], (DEPTH, D_MODEL), 0.05),
        'ln_b': nrm(ks[22], (DEPTH, D_MODEL), 0.02),
        'w_ple': nrm(ks[23], (DEPTH, D_PLE, D_MODEL), D_PLE ** -0.5),
        'w_ple_gate': nrm(ks[24], (DEPTH, D_MODEL, D_MODEL), D_MODEL ** -0.5),
        'b_ple_gate': nrm(ks[25], (DEPTH, D_MODEL), 0.02),
    }


def reference(x, p, w_in_e, b_in_e, conv_a_w, conv_a_b, ln_a_g, ln_a_b, ln_v_g, ln_v_b, w_s, b_s,
              w_out_e, b_out_e, w_in_o, b_in_o, w_pool, pool_scale, conv_d_w, w_out_o, b_out_o,
              ln_g, ln_b, w_ple, w_ple_gate, b_ple_gate):
    for i in range(DEPTH):
        j = i // 2
        if i % 2 == 0:
            out = even_mixer(x, w_in_e[j], b_in_e[j], conv_a_w[j], conv_a_b[j], ln_a_g[j], ln_a_b[j],
                             ln_v_g[j], ln_v_b[j], w_s[j], b_s[j], w_out_e[j], b_out_e[j])
        else:
            out = odd_mixer(x, w_in_o[j], b_in_o[j], w_pool[j], pool_scale[j], conv_d_w[j],
                            w_out_o[j], b_out_o[j])
        h = layer_norm(DEEPNORM_ALPHA * x + out, ln_g[i], ln_b[i])
        gate = jax.nn.sigmoid(jnp.einsum('bsd,de->bse', h, w_ple_gate[i]) + b_ple_gate[i])
        x = h + gate * jnp.einsum('bsk,kd->bsd', p[i], w_ple[i])
    return x
```

```python
import functools

import jax
import jax.numpy as jnp
from jax import lax
from jax.experimental import pallas as pl
from jax.experimental.pallas import tpu as pltpu

D_MODEL = 1024
DEPTH = 4
D_PLE = 256
W_BRANCH = D_MODEL // 2
CONV_A_WIDTH = 31
GMLP_BLOCK = 128
N_HEADS_B = 8
HEAD_DIM_B = W_BRANCH // N_HEADS_B
POOL_WINDOWS = (2, 4, 8, 16)
GROUP_DIM_C = W_BRANCH // len(POOL_WINDOWS)
CONV_D_WIDTH = 3
DEEPNORM_ALPHA = (2.0 * DEPTH) ** 0.25
LN_EPS = 1e-5

LANES = 128
SEQ_TILE = 256
HALO = 32
VMEM_LIMIT_BYTES = 48 * 1024 * 1024

_GELU_C1 = 0.7978845608028654
_GELU_C2 = 0.044715


def _sigmoid(x):
    return 0.5 * jnp.tanh(0.5 * x) + 0.5


def _silu(x):
    h = 0.5 * x
    return h * jnp.tanh(h) + h


def _gelu(x):
    inner = _GELU_C1 * (x + _GELU_C2 * (x * x * x))
    h = 0.5 * x
    return h * jnp.tanh(inner) + h


def _layer_norm(x, g, b):
    mu = jnp.mean(x, axis=-1, keepdims=True)
    d = x - mu
    var = jnp.mean(d * d, axis=-1, keepdims=True)
    return d * lax.rsqrt(var + LN_EPS) * g + b


def _proj(xb, w_ref, b_ref, col):
    sl = slice(col * W_BRANCH, (col + 1) * W_BRANCH)
    return jnp.dot(xb, w_ref[:, sl], preferred_element_type=jnp.float32) + b_ref[:, sl]


def _post(x, mix, p_ref, ln_g_ref, ln_b_ref, w_ple_ref, w_gate_ref, b_gate_ref, o_ref):
    h = _layer_norm(DEEPNORM_ALPHA * x + mix, ln_g_ref[...], ln_b_ref[...])
    gate = _sigmoid(
        jnp.dot(h.astype(jnp.bfloat16), w_gate_ref[...], preferred_element_type=jnp.float32)
        + b_gate_ref[...])
    ple = jnp.dot(p_ref[...].astype(jnp.bfloat16), w_ple_ref[...],
                  preferred_element_type=jnp.float32)
    o_ref[...] = h + gate * ple


def _even_kernel(x_ref, p_ref, w_in_ref, b_in_ref, conv_w_ref, conv_b_ref, ln_a_g_ref, ln_a_b_ref,
                 ln_v_g_ref, ln_v_b_ref, w_s_ref, b_s_ref, w_out_ref, b_out_ref,
                 ln_g_ref, ln_b_ref, w_ple_ref, w_gate_ref, b_gate_ref,
                 o_ref, a_ext_ref):
    tile = x_ref.shape[0]

    @pl.when(pl.program_id(1) == 0)
    def _():
        a_ext_ref[0:HALO, :] = jnp.zeros((HALO, W_BRANCH), jnp.float32)

    x = x_ref[...]
    xb = x.astype(jnp.bfloat16)

    a = _proj(xb, w_in_ref, b_in_ref, 0) * _sigmoid(_proj(xb, w_in_ref, b_in_ref, 1))
    a_ext_ref[HALO:HALO + tile, :] = a
    acc = jnp.broadcast_to(conv_b_ref[...], (tile, W_BRANCH))
    for k in range(CONV_A_WIDTH):
        off = HALO - (CONV_A_WIDTH - 1) + k
        acc = acc + conv_w_ref[k:k + 1, :] * a_ext_ref[off:off + tile, :]
    a_ext_ref[0:HALO, :] = a_ext_ref[tile:tile + HALO, :]
    a = _silu(_layer_norm(acc, ln_a_g_ref[...], ln_a_b_ref[...]))
    a = a * _silu(_proj(xb, w_in_ref, b_in_ref, 2))

    u = _gelu(_proj(xb, w_in_ref, b_in_ref, 3))
    v = _layer_norm(_gelu(_proj(xb, w_in_ref, b_in_ref, 4)), ln_v_g_ref[...], ln_v_b_ref[...])
    vb = v.astype(jnp.bfloat16)
    n_blocks = tile // GMLP_BLOCK
    row = lax.broadcasted_iota(jnp.int32, (GMLP_BLOCK, 2 * GMLP_BLOCK), 0)
    col = lax.broadcasted_iota(jnp.int32, (GMLP_BLOCK, 2 * GMLP_BLOCK), 1)
    tril = (col & (GMLP_BLOCK - 1)) <= row
    lane = lax.broadcasted_iota(jnp.int32, (GMLP_BLOCK, LANES), 1)
    first_head = lane < HEAD_DIM_B
    zero = jnp.zeros((GMLP_BLOCK, LANES), jnp.bfloat16)
    sg_cols = []
    for j in range(W_BRANCH // LANES):
        w_pair = jnp.where(tril, w_s_ref[j], jnp.zeros_like(w_s_ref[j]))
        rhs = []
        for n in range(n_blocks):
            vt = vb[n * GMLP_BLOCK:(n + 1) * GMLP_BLOCK, j * LANES:(j + 1) * LANES]
            rhs.append(jnp.concatenate(
                [jnp.where(first_head, vt, zero), jnp.where(first_head, zero, vt)], axis=0))
        rhs = jnp.concatenate(rhs, axis=1)
        mixed = jnp.dot(w_pair, rhs, preferred_element_type=jnp.float32)
        sg_cols.append(jnp.concatenate(
            [mixed[:, n * LANES:(n + 1) * LANES] for n in range(n_blocks)], axis=0))
    sg = jnp.concatenate(sg_cols, axis=1)
    b_s = jnp.concatenate([b_s_ref[...]] * n_blocks, axis=0)
    g = u * (sg + b_s) * _silu(_proj(xb, w_in_ref, b_in_ref, 5))

    y = jnp.concatenate([a, g], axis=-1).astype(jnp.bfloat16)
    mix = jnp.dot(y, w_out_ref[...], preferred_element_type=jnp.float32) + b_out_ref[...]
    _post(x, mix, p_ref, ln_g_ref, ln_b_ref, w_ple_ref, w_gate_ref, b_gate_ref, o_ref)


def _odd_kernel(x_ref, p_ref, w_in_ref, b_in_ref, w_pool_ref, pool_scale_ref, conv_w_ref,
                w_out_ref, b_out_ref, ln_g_ref, ln_b_ref, w_ple_ref, w_gate_ref, b_gate_ref,
                o_ref, c_ext_ref, d_ext_ref):
    tile = x_ref.shape[0]
    seq_tile_idx = pl.program_id(1)

    @pl.when(seq_tile_idx == 0)
    def _():
        c_ext_ref[0:HALO, :] = jnp.zeros((HALO, W_BRANCH), jnp.float32)
        d_ext_ref[0:HALO, :] = jnp.zeros((HALO, W_BRANCH), jnp.float32)

    x = x_ref[...]
    xb = x.astype(jnp.bfloat16)

    c_val = _proj(xb, w_in_ref, b_in_ref, 0)
    pos = (seq_tile_idx * tile + 1
           + lax.broadcasted_iota(jnp.int32, (tile, GROUP_DIM_C), 0)).astype(jnp.float32)
    c_ext_ref[HALO:HALO + tile, :] = c_val
    c_cols = []
    for gi, win in enumerate(POOL_WINDOWS):
        lanes = slice(gi * GROUP_DIM_C, (gi + 1) * GROUP_DIM_C)
        cur = c_val[:, lanes]
        wsum = cur
        for lag in range(1, win):
            wsum = wsum + c_ext_ref[HALO - lag:HALO - lag + tile, lanes]
        pooled = wsum / jnp.minimum(pos, float(win)) - cur
        c_cols.append(jnp.dot(pooled.astype(jnp.bfloat16), w_pool_ref[gi],
                              preferred_element_type=jnp.float32))
    c_ext_ref[0:HALO, :] = c_ext_ref[tile:tile + HALO, :]
    c = jnp.concatenate(c_cols, axis=-1) * pool_scale_ref[...]
    c = c * _silu(_proj(xb, w_in_ref, b_in_ref, 1))

    d_in = _proj(xb, w_in_ref, b_in_ref, 4) * _proj(xb, w_in_ref, b_in_ref, 2)
    d_ext_ref[HALO:HALO + tile, :] = d_in
    conv = conv_w_ref[CONV_D_WIDTH - 1:CONV_D_WIDTH, :] * d_in
    for k in range(CONV_D_WIDTH - 1):
        off = HALO - (CONV_D_WIDTH - 1) + k
        conv = conv + conv_w_ref[k:k + 1, :] * d_ext_ref[off:off + tile, :]
    d_ext_ref[0:HALO, :] = d_ext_ref[tile:tile + HALO, :]
    d = _proj(xb, w_in_ref, b_in_ref, 3) * conv * _silu(_proj(xb, w_in_ref, b_in_ref, 5))

    y = jnp.concatenate([c, d], axis=-1).astype(jnp.bfloat16)
    mix = jnp.dot(y, w_out_ref[...], preferred_element_type=jnp.float32) + b_out_ref[...]
    _post(x, mix, p_ref, ln_g_ref, ln_b_ref, w_ple_ref, w_gate_ref, b_gate_ref, o_ref)


def _full_spec(arr):
    zeros = (0,) * arr.ndim
    return pl.BlockSpec(arr.shape, lambda b, s: zeros)


def _run_layer(body, layer, x, p, params, n_scratch):
    batch, seq, _ = x.shape
    x_spec = pl.BlockSpec((None, SEQ_TILE, D_MODEL), lambda b, s: (b, s, 0))
    p_spec = pl.BlockSpec((None, None, SEQ_TILE, D_PLE), lambda b, s: (layer, b, s, 0))
    return pl.pallas_call(
        body,
        grid=(batch, seq // SEQ_TILE),
        in_specs=[x_spec, p_spec] + [_full_spec(a) for a in params],
        out_specs=x_spec,
        out_shape=jax.ShapeDtypeStruct(x.shape, x.dtype),
        scratch_shapes=[pltpu.VMEM((HALO + SEQ_TILE, W_BRANCH), jnp.float32)] * n_scratch,
        compiler_params=pltpu.CompilerParams(
            dimension_semantics=("arbitrary", "arbitrary"),
            vmem_limit_bytes=VMEM_LIMIT_BYTES),
        name=f"layer{layer}",
    )(x, p, *params)


def kernel(x, p, w_in_e, b_in_e, conv_a_w, conv_a_b, ln_a_g, ln_a_b, ln_v_g, ln_v_b, w_s, b_s,
           w_out_e, b_out_e, w_in_o, b_in_o, w_pool, pool_scale, conv_d_w, w_out_o, b_out_o,
           ln_g, ln_b, w_ple, w_ple_gate, b_ple_gate):
    assert x.shape[1] % SEQ_TILE == 0 and SEQ_TILE % GMLP_BLOCK == 0
    bf16 = jnp.bfloat16
    row = lambda v: v.reshape(1, -1)
    for i in range(DEPTH):
        j = i // 2
        tail = (row(ln_g[i]), row(ln_b[i]), w_ple[i].astype(bf16), w_ple_gate[i].astype(bf16),
                row(b_ple_gate[i]))
        if i % 2 == 0:
            w_pairs = w_s[j].reshape(N_HEADS_B // 2, 2, GMLP_BLOCK, GMLP_BLOCK)
            w_pairs = w_pairs.transpose(0, 2, 1, 3).reshape(N_HEADS_B // 2, GMLP_BLOCK, 2 * GMLP_BLOCK)
            b_s_full = jnp.repeat(b_s[j].T, HEAD_DIM_B, axis=1)
            params = (w_in_e[j].astype(bf16), row(b_in_e[j]), conv_a_w[j], row(conv_a_b[j]),
                      row(ln_a_g[j]), row(ln_a_b[j]), row(ln_v_g[j]), row(ln_v_b[j]),
                      w_pairs.astype(bf16), b_s_full, w_out_e[j].astype(bf16), row(b_out_e[j])) + tail
            x = _run_layer(_even_kernel, i, x, p, params, n_scratch=1)
        else:
            params = (w_in_o[j].astype(bf16), row(b_in_o[j]), w_pool[j].astype(bf16),
                      row(pool_scale[j]), conv_d_w[j], w_out_o[j].astype(bf16), row(b_out_o[j])) + tail
            x = _run_layer(_odd_kernel, i, x, p, params, n_scratch=2)
    return x
```

```python
import functools

import jax
import jax.numpy as jnp
from jax import lax
from jax.experimental import pallas as pl
from jax.experimental.pallas import tpu as pltpu

D_MODEL = 1024
DEPTH = 4
D_PLE = 256
W_BRANCH = D_MODEL // 2
CONV_A_WIDTH = 31
GMLP_BLOCK = 128
N_HEADS_B = 8
HEAD_DIM_B = W_BRANCH // N_HEADS_B
POOL_WINDOWS = (2, 4, 8, 16)
GROUP_DIM_C = W_BRANCH // len(POOL_WINDOWS)
CONV_D_WIDTH = 3
DEEPNORM_ALPHA = (2.0 * DEPTH) ** 0.25
LN_EPS = 1e-5

LANES = 128
SUBLANES = 8
SEQ_TILE = 512
HALO = 32
VMEM_LIMIT_BYTES = 56 * 1024 * 1024

N_EVEN_PARAMS = 17
N_ODD_PARAMS = 12

_GELU_C1 = 0.7978845608028654
_GELU_C2 = 0.044715


def _sigmoid(x):
    return 0.5 * jnp.tanh(0.5 * x) + 0.5


def _silu(x):
    h = 0.5 * x
    return h * jnp.tanh(h) + h


def _gelu(x):
    inner = _GELU_C1 * (x + _GELU_C2 * (x * x * x))
    h = 0.5 * x
    return h * jnp.tanh(inner) + h


def _layer_norm(x, g, b):
    mu = jnp.mean(x, axis=-1, keepdims=True)
    d = x - mu
    var = jnp.mean(d * d, axis=-1, keepdims=True)
    return d * lax.rsqrt(var + LN_EPS) * g + b


def _proj(xb, w_ref, b_ref, col):
    sl = slice(col * W_BRANCH, (col + 1) * W_BRANCH)
    return jnp.dot(xb, w_ref[col], preferred_element_type=jnp.float32) + b_ref[:, sl]


def _dot_slabs(lhs, w_ref):
    return jnp.concatenate(
        [jnp.dot(lhs, w_ref[i], preferred_element_type=jnp.float32) for i in range(w_ref.shape[0])],
        axis=-1)


def _ple(p, w_ple_ref):
    return _dot_slabs(p.astype(jnp.bfloat16), w_ple_ref)


def _deepnorm(x, mix, ln_g_ref, ln_b_ref):
    return _layer_norm(DEEPNORM_ALPHA * x + mix, ln_g_ref[...], ln_b_ref[...])


def _gated_embed(h, ple, w_gate_ref, b_gate_ref):
    gate = _sigmoid(_dot_slabs(h.astype(jnp.bfloat16), w_gate_ref) + b_gate_ref[...])
    return h + gate * ple


def _causal_conv_a(a_ext_ref, conv_w_ref, conv_b_ref, tile):
    a_ext = a_ext_ref[...]
    delayed = [a_ext] + [pltpu.roll(a_ext, j, axis=0) for j in range(1, 4)]
    partial = []
    for e in range(2):
        lo = HALO - SUBLANES * e
        acc = None
        for i in range(4):
            for j in range(4):
                s = 8 * i + 4 * e + j
                if s >= CONV_A_WIDTH:
                    continue
                k = CONV_A_WIDTH - 1 - s
                term = conv_w_ref[k:k + 1, :] * delayed[j][lo - 8 * i:HALO + tile - 8 * i, :]
                acc = term if acc is None else acc + term
        partial.append(acc)
    shifted = pltpu.roll(partial[1], 4, axis=0)[SUBLANES:, :]
    return partial[0] + shifted + conv_b_ref[...]


def _even_half(x, p, out, w_in_ref, b_in_ref, conv_w_ref, conv_b_ref, ln_a_g_ref, ln_a_b_ref,
               ln_v_g_ref, ln_v_b_ref, w_s_ref, b_s_ref, w_out_ref, b_out_ref,
               ln_g_ref, ln_b_ref, w_ple_ref, w_gate_ref, b_gate_ref, a_ext_ref):
    tile = x.shape[0]
    xb = x.astype(jnp.bfloat16)
    yield

    a_ext_ref[HALO:HALO + tile, :] = (
        _proj(xb, w_in_ref, b_in_ref, 0) * _sigmoid(_proj(xb, w_in_ref, b_in_ref, 1)))
    conv = _causal_conv_a(a_ext_ref, conv_w_ref, conv_b_ref, tile)
    a_ext_ref[0:HALO, :] = a_ext_ref[tile:tile + HALO, :]
    a = _silu(_layer_norm(conv, ln_a_g_ref[...], ln_a_b_ref[...]))
    yield

    a = a * _silu(_proj(xb, w_in_ref, b_in_ref, 2))
    u = _gelu(_proj(xb, w_in_ref, b_in_ref, 3))
    v = _layer_norm(_gelu(_proj(xb, w_in_ref, b_in_ref, 4)), ln_v_g_ref[...], ln_v_b_ref[...])
    g_gate = _silu(_proj(xb, w_in_ref, b_in_ref, 5))
    vb = v.astype(jnp.bfloat16)
    n_blocks = tile // GMLP_BLOCK
    row = lax.broadcasted_iota(jnp.int32, (GMLP_BLOCK, 2 * GMLP_BLOCK), 0)
    col = lax.broadcasted_iota(jnp.int32, (GMLP_BLOCK, 2 * GMLP_BLOCK), 1)
    tril = (col & (GMLP_BLOCK - 1)) <= row
    lane = lax.broadcasted_iota(jnp.int32, (GMLP_BLOCK, LANES), 1)
    first_head = lane < HEAD_DIM_B
    zero = jnp.zeros((GMLP_BLOCK, LANES), jnp.bfloat16)
    yield

    sg_cols = []
    for j in range(W_BRANCH // LANES):
        w_pair = jnp.where(tril, w_s_ref[j], jnp.zeros_like(w_s_ref[j]))
        rhs = []
        for n in range(n_blocks):
            vt = vb[n * GMLP_BLOCK:(n + 1) * GMLP_BLOCK, j * LANES:(j + 1) * LANES]
            rhs.append(jnp.concatenate(
                [jnp.where(first_head, vt, zero), jnp.where(first_head, zero, vt)], axis=0))
        rhs = jnp.concatenate(rhs, axis=1)
        mixed = jnp.dot(w_pair, rhs, preferred_element_type=jnp.float32)
        sg_cols.append(jnp.concatenate(
            [mixed[:, n * LANES:(n + 1) * LANES] for n in range(n_blocks)], axis=0))
    sg = jnp.concatenate(sg_cols, axis=1)
    b_s = jnp.concatenate([b_s_ref[...]] * n_blocks, axis=0)
    g = u * (sg + b_s) * g_gate
    y = jnp.concatenate([a, g], axis=-1).astype(jnp.bfloat16)
    yield

    ple = _ple(p, w_ple_ref)
    yield

    h = _deepnorm(x, _dot_slabs(y, w_out_ref) + b_out_ref[...], ln_g_ref, ln_b_ref)
    yield

    out(_gated_embed(h, ple, w_gate_ref, b_gate_ref))


def _odd_half(x, p, out, seq_tile_idx, w_in_ref, b_in_ref, w_pool_ref, pool_scale_ref, conv_w_ref,
              w_out_ref, b_out_ref, ln_g_ref, ln_b_ref, w_ple_ref, w_gate_ref, b_gate_ref,
              c_ext_ref, d_ext_ref):
    tile = x.shape[0]
    xb = x.astype(jnp.bfloat16)
    yield

    c_val = _proj(xb, w_in_ref, b_in_ref, 0)
    pos = (seq_tile_idx * tile + 1
           + lax.broadcasted_iota(jnp.int32, (tile, GROUP_DIM_C), 0)).astype(jnp.float32)
    c_ext_ref[HALO:HALO + tile, :] = c_val
    pooled = []
    for gi, win in enumerate(POOL_WINDOWS):
        lanes = slice(gi * GROUP_DIM_C, (gi + 1) * GROUP_DIM_C)
        wsum = c_ext_ref[:, lanes]
        span = 1
        while span < win:
            wsum = wsum + pltpu.roll(wsum, span, axis=0)
            span *= 2
        mean = wsum[HALO:, :] / jnp.minimum(pos, float(win))
        pooled.append((mean - c_val[:, lanes]).astype(jnp.bfloat16))
    c_ext_ref[0:HALO, :] = c_ext_ref[tile:tile + HALO, :]
    d_in = _proj(xb, w_in_ref, b_in_ref, 4) * _proj(xb, w_in_ref, b_in_ref, 2)
    d_ext_ref[HALO:HALO + tile, :] = d_in
    d_ext = d_ext_ref[...]
    conv = conv_w_ref[CONV_D_WIDTH - 1:CONV_D_WIDTH, :] * d_in
    for s in range(1, CONV_D_WIDTH):
        k = CONV_D_WIDTH - 1 - s
        conv = conv + conv_w_ref[k:k + 1, :] * pltpu.roll(d_ext, s, axis=0)[HALO:, :]
    d_ext_ref[0:HALO, :] = d_ext_ref[tile:tile + HALO, :]
    yield

    d = _proj(xb, w_in_ref, b_in_ref, 3) * conv * _silu(_proj(xb, w_in_ref, b_in_ref, 5))
    c_gate = _silu(_proj(xb, w_in_ref, b_in_ref, 1))
    yield

    c = jnp.concatenate(
        [jnp.dot(pooled[gi], w_pool_ref[gi], preferred_element_type=jnp.float32)
         for gi in range(len(POOL_WINDOWS))], axis=-1) * pool_scale_ref[...]
    y = jnp.concatenate([c * c_gate, d], axis=-1).astype(jnp.bfloat16)
    yield

    ple = _ple(p, w_ple_ref)
    yield

    h = _deepnorm(x, _dot_slabs(y, w_out_ref) + b_out_ref[...], ln_g_ref, ln_b_ref)
    yield

    out(_gated_embed(h, ple, w_gate_ref, b_gate_ref))


_MXU_ORDER = "EO" + "EOEOOEOOEEOE"


def _pair_kernel(n_seq_tiles, x_ref, p_even_ref, p_odd_ref, *refs):
    even = refs[:N_EVEN_PARAMS]
    odd = refs[N_EVEN_PARAMS:N_EVEN_PARAMS + N_ODD_PARAMS]
    o_ref, x_mid_ref, a_ext_ref, c_ext_ref, d_ext_ref = refs[N_EVEN_PARAMS + N_ODD_PARAMS:]
    step = pl.program_id(0)
    even_seq_idx = lax.rem(step, n_seq_tiles)
    odd_seq_idx = lax.rem(jnp.maximum(step - 1, 0), n_seq_tiles)
    slot = lax.rem(step, 2)
    halo_zeros = jnp.zeros((HALO, W_BRANCH), jnp.float32)

    @pl.when(step == 0)
    def _():
        def zero_rows(r, carry):
            x_mid_ref[1, pl.ds(pl.multiple_of(r * SUBLANES, SUBLANES), SUBLANES), :] = (
                jnp.zeros((SUBLANES, D_MODEL), jnp.float32))
            return carry
        lax.fori_loop(0, x_mid_ref.shape[1] // SUBLANES, zero_rows, 0)

    @pl.when(even_seq_idx == 0)
    def _():
        a_ext_ref[0:HALO, :] = halo_zeros

    @pl.when(odd_seq_idx == 0)
    def _():
        c_ext_ref[0:HALO, :] = halo_zeros
        d_ext_ref[0:HALO, :] = halo_zeros

    def write_mid(v):
        x_mid_ref[slot] = v

    def write_out(v):
        o_ref[...] = v

    halves = {
        "E": _even_half(x_ref[...], p_even_ref[...], write_mid, *even, a_ext_ref),
        "O": _odd_half(x_mid_ref[1 - slot], p_odd_ref[...], write_out, odd_seq_idx, *odd,
                       c_ext_ref, d_ext_ref),
    }
    for which in _MXU_ORDER:
        next(halves[which], None)


def _resident_spec(arr):
    zeros = (0,) * arr.ndim
    return pl.BlockSpec(arr.shape, lambda i: zeros, pipeline_mode=pl.Buffered(1))


def _run_pair(layer, x, p, even_params, odd_params):
    batch, seq, _ = x.shape
    assert len(even_params) == N_EVEN_PARAMS and len(odd_params) == N_ODD_PARAMS
    n_seq_tiles = seq // SEQ_TILE
    n_tiles = batch * n_seq_tiles

    def even_tile(i):
        t = jnp.minimum(i, n_tiles - 1)
        return t // n_seq_tiles, t % n_seq_tiles

    def odd_tile(i):
        t = jnp.maximum(i - 1, 0)
        return t // n_seq_tiles, t % n_seq_tiles

    x_block = (None, SEQ_TILE, D_MODEL)
    p_block = (None, None, SEQ_TILE, D_PLE)
    params = tuple(even_params) + tuple(odd_params)
    return pl.pallas_call(
        functools.partial(_pair_kernel, n_seq_tiles),
        grid=(n_tiles + 1,),
        in_specs=[pl.BlockSpec(x_block, lambda i: (*even_tile(i), 0)),
                  pl.BlockSpec(p_block, lambda i: (layer, *even_tile(i), 0)),
                  pl.BlockSpec(p_block, lambda i: (layer + 1, *odd_tile(i), 0))]
                 + [_resident_spec(a) for a in params],
        out_specs=pl.BlockSpec(x_block, lambda i: (*odd_tile(i), 0)),
        out_shape=jax.ShapeDtypeStruct(x.shape, x.dtype),
        scratch_shapes=[pltpu.VMEM((2, SEQ_TILE, D_MODEL), jnp.float32)]
                       + [pltpu.VMEM((HALO + SEQ_TILE, W_BRANCH), jnp.float32)] * 3,
        compiler_params=pltpu.CompilerParams(
            dimension_semantics=("arbitrary",),
            vmem_limit_bytes=VMEM_LIMIT_BYTES),
        name=f"layers{layer}{layer + 1}",
    )(x, p, p, *params)


def kernel(x, p, w_in_e, b_in_e, conv_a_w, conv_a_b, ln_a_g, ln_a_b, ln_v_g, ln_v_b, w_s, b_s,
           w_out_e, b_out_e, w_in_o, b_in_o, w_pool, pool_scale, conv_d_w, w_out_o, b_out_o,
           ln_g, ln_b, w_ple, w_ple_gate, b_ple_gate):
    assert x.shape[1] % SEQ_TILE == 0 and SEQ_TILE % GMLP_BLOCK == 0
    bf16 = jnp.bfloat16
    row = lambda v: v.reshape(1, -1)

    def slabs(w):
        k, n = w.shape
        return w.astype(bf16).reshape(k, n // W_BRANCH, W_BRANCH).transpose(1, 0, 2)

    def tail(i):
        return (row(ln_g[i]), row(ln_b[i]), slabs(w_ple[i]), slabs(w_ple_gate[i]),
                row(b_ple_gate[i]))

    for j in range(DEPTH // 2):
        w_pairs = w_s[j].reshape(N_HEADS_B // 2, 2, GMLP_BLOCK, GMLP_BLOCK)
        w_pairs = w_pairs.transpose(0, 2, 1, 3).reshape(N_HEADS_B // 2, GMLP_BLOCK, 2 * GMLP_BLOCK)
        b_s_full = jnp.repeat(b_s[j].T, HEAD_DIM_B, axis=1)
        even_params = (slabs(w_in_e[j]), row(b_in_e[j]), conv_a_w[j], row(conv_a_b[j]),
                       row(ln_a_g[j]), row(ln_a_b[j]), row(ln_v_g[j]), row(ln_v_b[j]),
                       w_pairs.astype(bf16), b_s_full, slabs(w_out_e[j]), row(b_out_e[j])
                       ) + tail(2 * j)
        odd_params = (slabs(w_in_o[j]), row(b_in_o[j]), w_pool[j].astype(bf16),
                      row(pool_scale[j]), conv_d_w[j], slabs(w_out_o[j]), row(b_out_o[j])
                      ) + tail(2 * j + 1)
        x = _run_pair(2 * j, x, p, even_params, odd_params)
    return x
```

```python
import functools

import jax
import jax.numpy as jnp
from jax import lax
from jax.experimental import pallas as pl
from jax.experimental.pallas import tpu as pltpu

D_MODEL = 1024
DEPTH = 4
D_PLE = 256
W_BRANCH = D_MODEL // 2
N_IN_BLOCKS = 6
CONV_A_WIDTH = 31
GMLP_BLOCK = 128
N_HEADS_B = 8
HEAD_DIM_B = W_BRANCH // N_HEADS_B
POOL_WINDOWS = (2, 4, 8, 16)
GROUP_DIM_C = W_BRANCH // len(POOL_WINDOWS)
CONV_D_WIDTH = 3
DEEPNORM_ALPHA = (2.0 * DEPTH) ** 0.25
LN_EPS = 1e-5

LANES = 128
SUBLANES = 8
SEQ_TILE = 512
HALO = 32
VMEM_LIMIT_BYTES = 56 * 1024 * 1024

N_PACKED_INPUTS = 6

_VECTOR_FIELDS = (
    ("b_in_e", N_IN_BLOCKS * W_BRANCH), ("conv_a_b", W_BRANCH), ("ln_a_g", W_BRANCH),
    ("ln_a_b", W_BRANCH), ("ln_v_g", W_BRANCH), ("ln_v_b", W_BRANCH), ("b_out_e", D_MODEL),
    ("ln_g_e", D_MODEL), ("ln_b_e", D_MODEL), ("b_gate_e", D_MODEL),
    ("b_in_o", N_IN_BLOCKS * W_BRANCH), ("pool_scale", W_BRANCH), ("b_out_o", D_MODEL),
    ("ln_g_o", D_MODEL), ("ln_b_o", D_MODEL), ("b_gate_o", D_MODEL))
_CONV_D_ROW = 32
_B_S_ROW = 40
_SLABS_PER_LAYER = N_IN_BLOCKS + 2 * (D_MODEL // W_BRANCH)

_GELU_C1 = 0.7978845608028654
_GELU_C2 = 0.044715


def _vector_views(vec_ref):
    views, off = {}, 0
    for name, width in _VECTOR_FIELDS:
        views[name] = vec_ref.at[:, off:off + width]
        off += width
    return views


def _layer_weight_views(w_big_ref, layer_in_pair):
    base = layer_in_pair * _SLABS_PER_LAYER
    n_out = D_MODEL // W_BRANCH
    return (w_big_ref.at[base:base + N_IN_BLOCKS],
            w_big_ref.at[base + N_IN_BLOCKS:base + N_IN_BLOCKS + n_out],
            w_big_ref.at[base + N_IN_BLOCKS + n_out:base + _SLABS_PER_LAYER])


def _silu_of_half(h):
    return h * jnp.tanh(h) + h


def _times_sigmoid_of_half(half_v, h):
    return half_v * jnp.tanh(h) + half_v


def _gelu(x):
    inner = x * (_GELU_C1 + (_GELU_C1 * _GELU_C2) * (x * x))
    h = 0.5 * x
    return h * jnp.tanh(inner) + h


def _layer_norm(x, g, b, eps=LN_EPS):
    mu = jnp.mean(x, axis=-1, keepdims=True)
    d = x - mu
    var = jnp.mean(d * d, axis=-1, keepdims=True)
    return d * lax.rsqrt(var + eps) * g + b


def _proj(xb, w_ref, b_ref, col):
    sl = slice(col * W_BRANCH, (col + 1) * W_BRANCH)
    return jnp.dot(xb, w_ref[col], preferred_element_type=jnp.float32) + b_ref[:, sl]


def _dot_slabs(lhs, w_ref):
    return jnp.concatenate(
        [jnp.dot(lhs, w_ref[i], preferred_element_type=jnp.float32) for i in range(w_ref.shape[0])],
        axis=-1)


def _half_ple(p, w_ple_ref):
    return _dot_slabs(p.astype(jnp.bfloat16), w_ple_ref)


def _deepnorm(x, mix_over_alpha, ln_g_ref, ln_b_ref):
    return _layer_norm(x + mix_over_alpha, ln_g_ref[...], ln_b_ref[...], LN_EPS / DEEPNORM_ALPHA ** 2)


def _gated_embed(h, half_ple, w_gate_ref, b_gate_ref):
    half_z = _dot_slabs(h.astype(jnp.bfloat16), w_gate_ref) + b_gate_ref[...]
    return h + _times_sigmoid_of_half(half_ple, half_z)


def _causal_conv_a(a_ext_ref, conv_w_ref, conv_b_ref, tile):
    a_ext = a_ext_ref[...]
    delayed = [a_ext] + [pltpu.roll(a_ext, j, axis=0) for j in range(1, 4)]
    partial = []
    for e in range(2):
        lo = HALO - SUBLANES * e
        acc = None
        for i in range(4):
            for j in range(4):
                s = 8 * i + 4 * e + j
                if s >= CONV_A_WIDTH:
                    continue
                k = CONV_A_WIDTH - 1 - s
                term = conv_w_ref[k:k + 1, :] * delayed[j][lo - 8 * i:HALO + tile - 8 * i, :]
                acc = term if acc is None else acc + term
        partial.append(acc)
    shifted = pltpu.roll(partial[1], 4, axis=0)[SUBLANES:, :]
    return partial[0] + shifted + conv_b_ref[...]


def _even_half(x, p, out, w_in_ref, b_in_ref, conv_w_ref, conv_b_ref, ln_a_g_ref, ln_a_b_ref,
               ln_v_g_ref, ln_v_b_ref, w_s_ref, b_s_ref, w_out_ref, b_out_ref,
               ln_g_ref, ln_b_ref, w_ple_ref, w_gate_ref, b_gate_ref, a_ext_ref):
    tile = x.shape[0]
    xb = x.astype(jnp.bfloat16)
    yield

    a_ext_ref[HALO:HALO + tile, :] = _times_sigmoid_of_half(
        _proj(xb, w_in_ref, b_in_ref, 0), _proj(xb, w_in_ref, b_in_ref, 1))
    conv = _causal_conv_a(a_ext_ref, conv_w_ref, conv_b_ref, tile)
    a_ext_ref[0:HALO, :] = a_ext_ref[tile:tile + HALO, :]
    a = _silu_of_half(_layer_norm(conv, ln_a_g_ref[...], ln_a_b_ref[...]))
    yield

    a = a * _silu_of_half(_proj(xb, w_in_ref, b_in_ref, 2))
    u = _gelu(_proj(xb, w_in_ref, b_in_ref, 3))
    v = _layer_norm(_gelu(_proj(xb, w_in_ref, b_in_ref, 4)), ln_v_g_ref[...], ln_v_b_ref[...])
    g_gate = _silu_of_half(_proj(xb, w_in_ref, b_in_ref, 5))
    vb = v.astype(jnp.bfloat16)
    n_blocks = tile // GMLP_BLOCK
    row = lax.broadcasted_iota(jnp.int32, (GMLP_BLOCK, 2 * GMLP_BLOCK), 0)
    col = lax.broadcasted_iota(jnp.int32, (GMLP_BLOCK, 2 * GMLP_BLOCK), 1)
    tril = (col & (GMLP_BLOCK - 1)) <= row
    lane = lax.broadcasted_iota(jnp.int32, (GMLP_BLOCK, LANES), 1)
    first_head = lane < HEAD_DIM_B
    zero = jnp.zeros((GMLP_BLOCK, LANES), jnp.bfloat16)
    yield

    sg_cols = []
    for j in range(W_BRANCH // LANES):
        w_pair = jnp.where(tril, w_s_ref[j], jnp.zeros_like(w_s_ref[j]))
        rhs = []
        for n in range(n_blocks):
            vt = vb[n * GMLP_BLOCK:(n + 1) * GMLP_BLOCK, j * LANES:(j + 1) * LANES]
            rhs.append(jnp.concatenate(
                [jnp.where(first_head, vt, zero), jnp.where(first_head, zero, vt)], axis=0))
        rhs = jnp.concatenate(rhs, axis=1)
        mixed = jnp.dot(w_pair, rhs, preferred_element_type=jnp.float32)
        sg_cols.append(jnp.concatenate(
            [mixed[:, n * LANES:(n + 1) * LANES] for n in range(n_blocks)], axis=0))
    sg = jnp.concatenate(sg_cols, axis=1)
    b_s = jnp.concatenate([b_s_ref[...]] * n_blocks, axis=0)
    g = u * (sg + b_s) * g_gate
    y = jnp.concatenate([a, g], axis=-1).astype(jnp.bfloat16)
    yield

    half_ple = _half_ple(p, w_ple_ref)
    yield

    h = _deepnorm(x, _dot_slabs(y, w_out_ref) + b_out_ref[...], ln_g_ref, ln_b_ref)
    yield

    out(_gated_embed(h, half_ple, w_gate_ref, b_gate_ref))


def _odd_half(x, p, out, seq_tile_idx, w_in_ref, b_in_ref, w_pool_ref, pool_scale_ref, conv_w_ref,
              w_out_ref, b_out_ref, ln_g_ref, ln_b_ref, w_ple_ref, w_gate_ref, b_gate_ref,
              c_ext_ref, d_ext_ref):
    tile = x.shape[0]
    xb = x.astype(jnp.bfloat16)
    yield

    c_val = _proj(xb, w_in_ref, b_in_ref, 0)
    pos = (seq_tile_idx * tile + 1
           + lax.broadcasted_iota(jnp.int32, (tile, GROUP_DIM_C), 0)).astype(jnp.float32)
    c_ext_ref[HALO:HALO + tile, :] = c_val
    pooled = []
    for gi, win in enumerate(POOL_WINDOWS):
        lanes = slice(gi * GROUP_DIM_C, (gi + 1) * GROUP_DIM_C)
        wsum = c_ext_ref[:, lanes]
        span = 1
        while span < win:
            wsum = wsum + pltpu.roll(wsum, span, axis=0)
            span *= 2
        mean = wsum[HALO:, :] / jnp.minimum(pos, float(win))
        pooled.append((mean - c_val[:, lanes]).astype(jnp.bfloat16))
    c_ext_ref[0:HALO, :] = c_ext_ref[tile:tile + HALO, :]
    d_in = _proj(xb, w_in_ref, b_in_ref, 4) * _proj(xb, w_in_ref, b_in_ref, 2)
    d_ext_ref[HALO:HALO + tile, :] = d_in
    d_ext = d_ext_ref[...]
    conv = conv_w_ref[CONV_D_WIDTH - 1:CONV_D_WIDTH, :] * d_in
    for s in range(1, CONV_D_WIDTH):
        k = CONV_D_WIDTH - 1 - s
        conv = conv + conv_w_ref[k:k + 1, :] * pltpu.roll(d_ext, s, axis=0)[HALO:, :]
    d_ext_ref[0:HALO, :] = d_ext_ref[tile:tile + HALO, :]
    yield

    d = _proj(xb, w_in_ref, b_in_ref, 3) * conv * _silu_of_half(_proj(xb, w_in_ref, b_in_ref, 5))
    c_gate = _silu_of_half(_proj(xb, w_in_ref, b_in_ref, 1))
    yield

    c = jnp.concatenate(
        [jnp.dot(pooled[gi], w_pool_ref[gi], preferred_element_type=jnp.float32)
         for gi in range(len(POOL_WINDOWS))], axis=-1) * pool_scale_ref[...]
    y = jnp.concatenate([c * c_gate, d], axis=-1).astype(jnp.bfloat16)
    yield

    half_ple = _half_ple(p, w_ple_ref)
    yield

    h = _deepnorm(x, _dot_slabs(y, w_out_ref) + b_out_ref[...], ln_g_ref, ln_b_ref)
    yield

    out(_gated_embed(h, half_ple, w_gate_ref, b_gate_ref))


_MXU_ORDER = "EO" + "EOEOOEOOEEOE"


def _pair_kernel(n_seq_tiles, x_ref, p_even_ref, p_odd_ref, *refs):
    w_big_ref, w_ple_ref, w_s_ref, w_pool_ref, table_ref, vec_ref = refs[:N_PACKED_INPUTS]
    o_ref, x_mid_ref, a_ext_ref, c_ext_ref, d_ext_ref = refs[N_PACKED_INPUTS:]
    vec = _vector_views(vec_ref)
    w_in_e, w_out_e, w_gate_e = _layer_weight_views(w_big_ref, 0)
    w_in_o, w_out_o, w_gate_o = _layer_weight_views(w_big_ref, 1)
    n_ple = D_MODEL // W_BRANCH
    even = (w_in_e, vec["b_in_e"], table_ref.at[0:_CONV_D_ROW], vec["conv_a_b"],
            vec["ln_a_g"], vec["ln_a_b"], vec["ln_v_g"], vec["ln_v_b"], w_s_ref,
            table_ref.at[_B_S_ROW:_B_S_ROW + GMLP_BLOCK], w_out_e, vec["b_out_e"],
            vec["ln_g_e"], vec["ln_b_e"], w_ple_ref.at[0:n_ple], w_gate_e, vec["b_gate_e"])
    odd = (w_in_o, vec["b_in_o"], w_pool_ref, vec["pool_scale"],
           table_ref.at[_CONV_D_ROW:_B_S_ROW], w_out_o, vec["b_out_o"],
           vec["ln_g_o"], vec["ln_b_o"], w_ple_ref.at[n_ple:2 * n_ple], w_gate_o, vec["b_gate_o"])
    step = pl.program_id(0)
    even_seq_idx = lax.rem(step, n_seq_tiles)
    odd_seq_idx = lax.rem(jnp.maximum(step - 1, 0), n_seq_tiles)
    slot = lax.rem(step, 2)
    halo_zeros = jnp.zeros((HALO, W_BRANCH), jnp.float32)

    @pl.when(step == 0)
    def _():
        def zero_rows(r, carry):
            x_mid_ref[1, pl.ds(pl.multiple_of(r * SUBLANES, SUBLANES), SUBLANES), :] = (
                jnp.zeros((SUBLANES, D_MODEL), jnp.float32))
            return carry
        lax.fori_loop(0, x_mid_ref.shape[1] // SUBLANES, zero_rows, 0)

    @pl.when(even_seq_idx == 0)
    def _():
        a_ext_ref[0:HALO, :] = halo_zeros

    @pl.when(odd_seq_idx == 0)
    def _():
        c_ext_ref[0:HALO, :] = halo_zeros
        d_ext_ref[0:HALO, :] = halo_zeros

    def write_mid(v):
        x_mid_ref[slot] = v

    def write_out(v):
        o_ref[...] = v

    halves = {
        "E": _even_half(x_ref[...], p_even_ref[...], write_mid, *even, a_ext_ref),
        "O": _odd_half(x_mid_ref[1 - slot], p_odd_ref[...], write_out, odd_seq_idx, *odd,
                       c_ext_ref, d_ext_ref),
    }
    for which in _MXU_ORDER:
        next(halves[which], None)
    assert all(next(h, "done") == "done" for h in halves.values()), "_MXU_ORDER leaves work unemitted"


def _resident_spec(arr):
    zeros = (0,) * arr.ndim
    return pl.BlockSpec(arr.shape, lambda i: zeros, pipeline_mode=pl.Buffered(1))


def _run_pair(layer, x, p, packed):
    batch, seq, _ = x.shape
    assert len(packed) == N_PACKED_INPUTS
    n_seq_tiles = seq // SEQ_TILE
    n_tiles = batch * n_seq_tiles

    def even_tile(i):
        t = jnp.minimum(i, n_tiles - 1)
        return t // n_seq_tiles, t % n_seq_tiles

    def odd_tile(i):
        t = jnp.maximum(i - 1, 0)
        return t // n_seq_tiles, t % n_seq_tiles

    x_block = (None, SEQ_TILE, D_MODEL)
    p_block = (None, None, SEQ_TILE, D_PLE)
    return pl.pallas_call(
        functools.partial(_pair_kernel, n_seq_tiles),
        grid=(n_tiles + 1,),
        in_specs=[pl.BlockSpec(x_block, lambda i: (*even_tile(i), 0)),
                  pl.BlockSpec(p_block, lambda i: (layer, *even_tile(i), 0)),
                  pl.BlockSpec(p_block, lambda i: (layer + 1, *odd_tile(i), 0))]
                 + [_resident_spec(a) for a in packed],
        out_specs=pl.BlockSpec(x_block, lambda i: (*odd_tile(i), 0)),
        out_shape=jax.ShapeDtypeStruct(x.shape, x.dtype),
        scratch_shapes=[pltpu.VMEM((2, SEQ_TILE, D_MODEL), jnp.float32)]
                       + [pltpu.VMEM((HALO + SEQ_TILE, W_BRANCH), jnp.float32)] * 3,
        compiler_params=pltpu.CompilerParams(
            dimension_semantics=("arbitrary",),
            vmem_limit_bytes=VMEM_LIMIT_BYTES),
        name=f"layers{layer}{layer + 1}",
    )(x, p, p, *packed)


def kernel(x, p, w_in_e, b_in_e, conv_a_w, conv_a_b, ln_a_g, ln_a_b, ln_v_g, ln_v_b, w_s, b_s,
           w_out_e, b_out_e, w_in_o, b_in_o, w_pool, pool_scale, conv_d_w, w_out_o, b_out_o,
           ln_g, ln_b, w_ple, w_ple_gate, b_ple_gate):
    assert x.shape[1] % SEQ_TILE == 0 and SEQ_TILE % GMLP_BLOCK == 0
    bf16 = jnp.bfloat16
    row = lambda v: v.reshape(1, -1)

    def slabs(w, col_scale):
        k, n = w.shape
        return (w * col_scale).astype(bf16).reshape(k, n // W_BRANCH, W_BRANCH).transpose(1, 0, 2)

    def block_scale(scales):
        return jnp.repeat(jnp.asarray(scales, jnp.float32), W_BRANCH)[None, :]

    def pad_rows(a, rows):
        return jnp.pad(a, ((0, rows - a.shape[0]), (0, 0)))

    even_scale = block_scale((0.5, 0.5, 0.5, 1.0, 1.0, 0.5))
    odd_scale = block_scale((1.0, 0.5, 1.0, 1.0, 1.0, 0.5))
    inv_alpha = 1.0 / DEEPNORM_ALPHA

    for j in range(DEPTH // 2):
        e, o = 2 * j, 2 * j + 1
        w_pairs = w_s[j].reshape(N_HEADS_B // 2, 2, GMLP_BLOCK, GMLP_BLOCK)
        w_pairs = w_pairs.transpose(0, 2, 1, 3).reshape(N_HEADS_B // 2, GMLP_BLOCK, 2 * GMLP_BLOCK)
        b_s_full = jnp.repeat(b_s[j].T, HEAD_DIM_B, axis=1)
        w_big = jnp.concatenate(
            [slabs(w_in_e[j], even_scale), slabs(w_out_e[j], inv_alpha), slabs(w_ple_gate[e], 0.5),
             slabs(w_in_o[j], odd_scale), slabs(w_out_o[j], inv_alpha), slabs(w_ple_gate[o], 0.5)])
        w_ple_both = jnp.concatenate([slabs(w_ple[e], 0.5), slabs(w_ple[o], 0.5)])
        table = jnp.concatenate([pad_rows(conv_a_w[j], _CONV_D_ROW),
                                 pad_rows(conv_d_w[j], _B_S_ROW - _CONV_D_ROW), b_s_full])
        vectors = {
            "b_in_e": row(b_in_e[j]) * even_scale, "conv_a_b": row(conv_a_b[j]),
            "ln_a_g": row(0.5 * ln_a_g[j]), "ln_a_b": row(0.5 * ln_a_b[j]),
            "ln_v_g": row(ln_v_g[j]), "ln_v_b": row(ln_v_b[j]),
            "b_out_e": row(inv_alpha * b_out_e[j]), "ln_g_e": row(ln_g[e]), "ln_b_e": row(ln_b[e]),
            "b_gate_e": row(0.5 * b_ple_gate[e]),
            "b_in_o": row(b_in_o[j]) * odd_scale, "pool_scale": row(pool_scale[j]),
            "b_out_o": row(inv_alpha * b_out_o[j]), "ln_g_o": row(ln_g[o]), "ln_b_o": row(ln_b[o]),
            "b_gate_o": row(0.5 * b_ple_gate[o]),
        }
        vec = jnp.concatenate([vectors[name] for name, _ in _VECTOR_FIELDS], axis=1)
        x = _run_pair(e, x, p, (w_big, w_ple_both, w_pairs.astype(bf16), w_pool[j].astype(bf16),
                                table, vec))
    return x
```

```python
import functools

import jax
import jax.numpy as jnp
from jax import lax
from jax.experimental import pallas as pl
from jax.experimental.pallas import tpu as pltpu

D_MODEL = 1024
DEPTH = 4
D_PLE = 256
W_BRANCH = D_MODEL // 2
N_IN_BLOCKS = 6
CONV_A_WIDTH = 31
GMLP_BLOCK = 128
N_HEADS_B = 8
HEAD_DIM_B = W_BRANCH // N_HEADS_B
POOL_WINDOWS = (2, 4, 8, 16)
GROUP_DIM_C = W_BRANCH // len(POOL_WINDOWS)
CONV_D_WIDTH = 3
DEEPNORM_ALPHA = (2.0 * DEPTH) ** 0.25
LN_EPS = 1e-5

LANES = 128
SUBLANES = 8
SEQ_TILE = 512
HALO = 32
VMEM_LIMIT_BYTES = 56 * 1024 * 1024

N_RESIDENT_INPUTS = 12

_VECTOR_FIELDS = (
    ("b_in_e", N_IN_BLOCKS * W_BRANCH), ("conv_a_b", W_BRANCH), ("ln_a_g", W_BRANCH),
    ("ln_a_b", W_BRANCH), ("ln_v_g", W_BRANCH), ("ln_v_b", W_BRANCH), ("b_out_e", D_MODEL),
    ("ln_g_e", D_MODEL), ("ln_b_e", D_MODEL), ("b_gate_e", D_MODEL),
    ("b_in_o", N_IN_BLOCKS * W_BRANCH), ("pool_scale", W_BRANCH), ("b_out_o", D_MODEL),
    ("ln_g_o", D_MODEL), ("ln_b_o", D_MODEL), ("b_gate_o", D_MODEL))
_CONV_D_ROW = 32
_B_S_ROW = 40

_GELU_C1 = 0.7978845608028654
_GELU_C2 = 0.044715


def _vector_views(vec_ref):
    views, off = {}, 0
    for name, width in _VECTOR_FIELDS:
        views[name] = vec_ref.at[:, off:off + width]
        off += width
    return views


def _slab_kernel(w_ref, scale_ref, o_ref):
    o_ref[...] = (w_ref[...] * scale_ref[...]).astype(o_ref.dtype)


def _to_slabs(w, col_scale):
    n_layers, k, n = w.shape
    n_slabs = n // W_BRANCH
    return pl.pallas_call(
        _slab_kernel,
        grid=(n_layers, n_slabs),
        in_specs=[pl.BlockSpec((None, k, W_BRANCH), lambda l, s: (l, 0, s)),
                  pl.BlockSpec((1, W_BRANCH), lambda l, s: (0, s))],
        out_specs=pl.BlockSpec((None, None, k, W_BRANCH), lambda l, s: (l, s, 0, 0)),
        out_shape=jax.ShapeDtypeStruct((n_layers, n_slabs, k, W_BRANCH), jnp.bfloat16),
        compiler_params=pltpu.CompilerParams(dimension_semantics=("arbitrary", "arbitrary")),
        name="weight_slabs",
    )(w, col_scale)


def _silu_of_half(h):
    return h * jnp.tanh(h) + h


def _times_sigmoid_of_half(half_v, h):
    return half_v * jnp.tanh(h) + half_v


def _gelu(x):
    inner = x * (_GELU_C1 + (_GELU_C1 * _GELU_C2) * (x * x))
    h = 0.5 * x
    return h * jnp.tanh(inner) + h


def _layer_norm(x, g, b, eps=LN_EPS):
    mu = jnp.mean(x, axis=-1, keepdims=True)
    d = x - mu
    var = jnp.mean(d * d, axis=-1, keepdims=True)
    return d * lax.rsqrt(var + eps) * g + b


def _proj(xb, w_ref, b_ref, col):
    sl = slice(col * W_BRANCH, (col + 1) * W_BRANCH)
    return jnp.dot(xb, w_ref[col], preferred_element_type=jnp.float32) + b_ref[:, sl]


def _dot_slabs(lhs, w_ref):
    return jnp.concatenate(
        [jnp.dot(lhs, w_ref[i], preferred_element_type=jnp.float32) for i in range(w_ref.shape[0])],
        axis=-1)


def _half_ple(p, w_ple_ref):
    return _dot_slabs(p.astype(jnp.bfloat16), w_ple_ref)


def _deepnorm(x, mix_over_alpha, ln_g_ref, ln_b_ref):
    return _layer_norm(x + mix_over_alpha, ln_g_ref[...], ln_b_ref[...], LN_EPS / DEEPNORM_ALPHA ** 2)


def _gated_embed(h, half_ple, w_gate_ref, b_gate_ref):
    half_z = _dot_slabs(h.astype(jnp.bfloat16), w_gate_ref) + b_gate_ref[...]
    return h + _times_sigmoid_of_half(half_ple, half_z)


def _causal_conv_a(a_ext_ref, conv_w_ref, conv_b_ref, tile):
    a_ext = a_ext_ref[...]
    delayed = [a_ext] + [pltpu.roll(a_ext, j, axis=0) for j in range(1, 4)]
    partial = []
    for e in range(2):
        lo = HALO - SUBLANES * e
        acc = None
        for i in range(4):
            for j in range(4):
                s = 8 * i + 4 * e + j
                if s >= CONV_A_WIDTH:
                    continue
                k = CONV_A_WIDTH - 1 - s
                term = conv_w_ref[k:k + 1, :] * delayed[j][lo - 8 * i:HALO + tile - 8 * i, :]
                acc = term if acc is None else acc + term
        partial.append(acc)
    shifted = pltpu.roll(partial[1], 4, axis=0)[SUBLANES:, :]
    return partial[0] + shifted + conv_b_ref[...]


def _even_half(x, p, out, w_in_ref, b_in_ref, conv_w_ref, conv_b_ref, ln_a_g_ref, ln_a_b_ref,
               ln_v_g_ref, ln_v_b_ref, w_s_ref, b_s_ref, w_out_ref, b_out_ref,
               ln_g_ref, ln_b_ref, w_ple_ref, w_gate_ref, b_gate_ref, a_ext_ref):
    tile = x.shape[0]
    xb = x.astype(jnp.bfloat16)
    yield

    a_ext_ref[HALO:HALO + tile, :] = _times_sigmoid_of_half(
        _proj(xb, w_in_ref, b_in_ref, 0), _proj(xb, w_in_ref, b_in_ref, 1))
    conv = _causal_conv_a(a_ext_ref, conv_w_ref, conv_b_ref, tile)
    a_ext_ref[0:HALO, :] = a_ext_ref[tile:tile + HALO, :]
    a = _silu_of_half(_layer_norm(conv, ln_a_g_ref[...], ln_a_b_ref[...]))
    yield

    a = a * _silu_of_half(_proj(xb, w_in_ref, b_in_ref, 2))
    u = _gelu(_proj(xb, w_in_ref, b_in_ref, 3))
    v = _layer_norm(_gelu(_proj(xb, w_in_ref, b_in_ref, 4)), ln_v_g_ref[...], ln_v_b_ref[...])
    g_gate = _silu_of_half(_proj(xb, w_in_ref, b_in_ref, 5))
    vb = v.astype(jnp.bfloat16)
    n_blocks = tile // GMLP_BLOCK
    row = lax.broadcasted_iota(jnp.int32, (GMLP_BLOCK, 2 * GMLP_BLOCK), 0)
    col = lax.broadcasted_iota(jnp.int32, (GMLP_BLOCK, 2 * GMLP_BLOCK), 1)
    tril = (col & (GMLP_BLOCK - 1)) <= row
    lane = lax.broadcasted_iota(jnp.int32, (GMLP_BLOCK, LANES), 1)
    first_head = lane < HEAD_DIM_B
    zero = jnp.zeros((GMLP_BLOCK, LANES), jnp.bfloat16)
    yield

    sg_cols = []
    for j in range(W_BRANCH // LANES):
        w_pair = jnp.where(tril, w_s_ref[j], jnp.zeros_like(w_s_ref[j]))
        rhs = []
        for n in range(n_blocks):
            vt = vb[n * GMLP_BLOCK:(n + 1) * GMLP_BLOCK, j * LANES:(j + 1) * LANES]
            rhs.append(jnp.concatenate(
                [jnp.where(first_head, vt, zero), jnp.where(first_head, zero, vt)], axis=0))
        rhs = jnp.concatenate(rhs, axis=1)
        mixed = jnp.dot(w_pair, rhs, preferred_element_type=jnp.float32)
        sg_cols.append(jnp.concatenate(
            [mixed[:, n * LANES:(n + 1) * LANES] for n in range(n_blocks)], axis=0))
    sg = jnp.concatenate(sg_cols, axis=1)
    b_s = jnp.concatenate([b_s_ref[...]] * n_blocks, axis=0)
    g = u * (sg + b_s) * g_gate
    y = jnp.concatenate([a, g], axis=-1).astype(jnp.bfloat16)
    yield

    half_ple = _half_ple(p, w_ple_ref)
    yield

    h = _deepnorm(x, _dot_slabs(y, w_out_ref) + b_out_ref[...], ln_g_ref, ln_b_ref)
    yield

    out(_gated_embed(h, half_ple, w_gate_ref, b_gate_ref))


def _odd_half(x, p, out, seq_tile_idx, w_in_ref, b_in_ref, w_pool_ref, pool_scale_ref, conv_w_ref,
              w_out_ref, b_out_ref, ln_g_ref, ln_b_ref, w_ple_ref, w_gate_ref, b_gate_ref,
              c_ext_ref, d_ext_ref):
    tile = x.shape[0]
    xb = x.astype(jnp.bfloat16)
    yield

    c_val = _proj(xb, w_in_ref, b_in_ref, 0)
    pos = (seq_tile_idx * tile + 1
           + lax.broadcasted_iota(jnp.int32, (tile, GROUP_DIM_C), 0)).astype(jnp.float32)
    c_ext_ref[HALO:HALO + tile, :] = c_val
    pooled = []
    for gi, win in enumerate(POOL_WINDOWS):
        lanes = slice(gi * GROUP_DIM_C, (gi + 1) * GROUP_DIM_C)
        wsum = c_ext_ref[:, lanes]
        span = 1
        while span < win:
            wsum = wsum + pltpu.roll(wsum, span, axis=0)
            span *= 2
        mean = wsum[HALO:, :] / jnp.minimum(pos, float(win))
        pooled.append((mean - c_val[:, lanes]).astype(jnp.bfloat16))
    c_ext_ref[0:HALO, :] = c_ext_ref[tile:tile + HALO, :]
    d_in = _proj(xb, w_in_ref, b_in_ref, 4) * _proj(xb, w_in_ref, b_in_ref, 2)
    d_ext_ref[HALO:HALO + tile, :] = d_in
    d_ext = d_ext_ref[...]
    conv = conv_w_ref[CONV_D_WIDTH - 1:CONV_D_WIDTH, :] * d_in
    for s in range(1, CONV_D_WIDTH):
        k = CONV_D_WIDTH - 1 - s
        conv = conv + conv_w_ref[k:k + 1, :] * pltpu.roll(d_ext, s, axis=0)[HALO:, :]
    d_ext_ref[0:HALO, :] = d_ext_ref[tile:tile + HALO, :]
    yield

    d = _proj(xb, w_in_ref, b_in_ref, 3) * conv * _silu_of_half(_proj(xb, w_in_ref, b_in_ref, 5))
    c_gate = _silu_of_half(_proj(xb, w_in_ref, b_in_ref, 1))
    yield

    c = jnp.concatenate(
        [jnp.dot(pooled[gi], w_pool_ref[gi], preferred_element_type=jnp.float32)
         for gi in range(len(POOL_WINDOWS))], axis=-1) * pool_scale_ref[...]
    y = jnp.concatenate([c * c_gate, d], axis=-1).astype(jnp.bfloat16)
    yield

    half_ple = _half_ple(p, w_ple_ref)
    yield

    h = _deepnorm(x, _dot_slabs(y, w_out_ref) + b_out_ref[...], ln_g_ref, ln_b_ref)
    yield

    out(_gated_embed(h, half_ple, w_gate_ref, b_gate_ref))


_MXU_ORDER = "EO" + "EOEOOEOOEEOE"


def _pair_kernel(n_seq_tiles, x_ref, p_even_ref, p_odd_ref, *refs):
    (w_in_e, w_out_e, w_gate_e, w_ple_e, w_in_o, w_out_o, w_gate_o, w_ple_o,
     w_s_ref, w_pool_ref, table_ref, vec_ref) = refs[:N_RESIDENT_INPUTS]
    o_ref, x_mid_ref, a_ext_ref, c_ext_ref, d_ext_ref = refs[N_RESIDENT_INPUTS:]
    vec = _vector_views(vec_ref)
    even = (w_in_e, vec["b_in_e"], table_ref.at[0:_CONV_D_ROW], vec["conv_a_b"],
            vec["ln_a_g"], vec["ln_a_b"], vec["ln_v_g"], vec["ln_v_b"], w_s_ref,
            table_ref.at[_B_S_ROW:_B_S_ROW + GMLP_BLOCK], w_out_e, vec["b_out_e"],
            vec["ln_g_e"], vec["ln_b_e"], w_ple_e, w_gate_e, vec["b_gate_e"])
    odd = (w_in_o, vec["b_in_o"], w_pool_ref, vec["pool_scale"],
           table_ref.at[_CONV_D_ROW:_B_S_ROW], w_out_o, vec["b_out_o"],
           vec["ln_g_o"], vec["ln_b_o"], w_ple_o, w_gate_o, vec["b_gate_o"])
    step = pl.program_id(0)
    even_seq_idx = lax.rem(step, n_seq_tiles)
    odd_seq_idx = lax.rem(jnp.maximum(step - 1, 0), n_seq_tiles)
    slot = lax.rem(step, 2)
    halo_zeros = jnp.zeros((HALO, W_BRANCH), jnp.float32)

    @pl.when(step == 0)
    def _():
        def zero_rows(r, carry):
            x_mid_ref[1, pl.ds(pl.multiple_of(r * SUBLANES, SUBLANES), SUBLANES), :] = (
                jnp.zeros((SUBLANES, D_MODEL), jnp.float32))
            return carry
        lax.fori_loop(0, x_mid_ref.shape[1] // SUBLANES, zero_rows, 0)

    @pl.when(even_seq_idx == 0)
    def _():
        a_ext_ref[0:HALO, :] = halo_zeros

    @pl.when(odd_seq_idx == 0)
    def _():
        c_ext_ref[0:HALO, :] = halo_zeros
        d_ext_ref[0:HALO, :] = halo_zeros

    def write_mid(v):
        x_mid_ref[slot] = v

    def write_out(v):
        o_ref[...] = v

    halves = {
        "E": _even_half(x_ref[...], p_even_ref[...], write_mid, *even, a_ext_ref),
        "O": _odd_half(x_mid_ref[1 - slot], p_odd_ref[...], write_out, odd_seq_idx, *odd,
                       c_ext_ref, d_ext_ref),
    }
    for which in _MXU_ORDER:
        next(halves[which], None)
    assert all(next(h, "done") == "done" for h in halves.values()), "_MXU_ORDER leaves work unemitted"


def _layer_spec(arr, index):
    zeros = (0,) * (arr.ndim - 1)
    return pl.BlockSpec((None,) + arr.shape[1:], lambda i: (index,) + zeros,
                        pipeline_mode=pl.Buffered(1))


def _run_pair(pair, x, p, resident):
    batch, seq, _ = x.shape
    assert len(resident) == N_RESIDENT_INPUTS
    layer = 2 * pair
    n_seq_tiles = seq // SEQ_TILE
    n_tiles = batch * n_seq_tiles

    def even_tile(i):
        t = jnp.minimum(i, n_tiles - 1)
        return t // n_seq_tiles, t % n_seq_tiles

    def odd_tile(i):
        t = jnp.maximum(i - 1, 0)
        return t // n_seq_tiles, t % n_seq_tiles

    x_block = (None, SEQ_TILE, D_MODEL)
    p_block = (None, None, SEQ_TILE, D_PLE)
    return pl.pallas_call(
        functools.partial(_pair_kernel, n_seq_tiles),
        grid=(n_tiles + 1,),
        in_specs=[pl.BlockSpec(x_block, lambda i: (*even_tile(i), 0)),
                  pl.BlockSpec(p_block, lambda i: (layer, *even_tile(i), 0)),
                  pl.BlockSpec(p_block, lambda i: (layer + 1, *odd_tile(i), 0))]
                 + [_layer_spec(arr, index) for arr, index in resident],
        out_specs=pl.BlockSpec(x_block, lambda i: (*odd_tile(i), 0)),
        out_shape=jax.ShapeDtypeStruct(x.shape, x.dtype),
        scratch_shapes=[pltpu.VMEM((2, SEQ_TILE, D_MODEL), jnp.float32)]
                       + [pltpu.VMEM((HALO + SEQ_TILE, W_BRANCH), jnp.float32)] * 3,
        compiler_params=pltpu.CompilerParams(
            dimension_semantics=("arbitrary",),
            vmem_limit_bytes=VMEM_LIMIT_BYTES),
        name=f"layers{layer}{layer + 1}",
    )(x, p, p, *[arr for arr, _ in resident])


def kernel(x, p, w_in_e, b_in_e, conv_a_w, conv_a_b, ln_a_g, ln_a_b, ln_v_g, ln_v_b, w_s, b_s,
           w_out_e, b_out_e, w_in_o, b_in_o, w_pool, pool_scale, conv_d_w, w_out_o, b_out_o,
           ln_g, ln_b, w_ple, w_ple_gate, b_ple_gate):
    assert x.shape[1] % SEQ_TILE == 0 and SEQ_TILE % GMLP_BLOCK == 0
    n_pairs = DEPTH // 2
    f32 = jnp.float32

    def block_scale(scales):
        return jnp.repeat(jnp.asarray(scales, f32), W_BRANCH)[None, :]

    def pad_rows(a, rows):
        return jnp.pad(a, ((0, 0), (0, rows - a.shape[1]), (0, 0)))

    even_scale = block_scale((0.5, 0.5, 0.5, 1.0, 1.0, 0.5))
    odd_scale = block_scale((1.0, 0.5, 1.0, 1.0, 1.0, 0.5))
    inv_alpha = 1.0 / DEEPNORM_ALPHA
    half_row = jnp.full((1, D_MODEL), 0.5, f32)
    inv_alpha_row = jnp.full((1, D_MODEL), inv_alpha, f32)

    w_in_e_s = _to_slabs(w_in_e, even_scale)
    w_in_o_s = _to_slabs(w_in_o, odd_scale)
    w_out_e_s = _to_slabs(w_out_e, inv_alpha_row)
    w_out_o_s = _to_slabs(w_out_o, inv_alpha_row)
    w_gate_s = _to_slabs(w_ple_gate, half_row)
    w_ple_s = _to_slabs(w_ple, half_row)

    w_pairs = w_s.reshape(n_pairs, N_HEADS_B // 2, 2, GMLP_BLOCK, GMLP_BLOCK)
    w_pairs = w_pairs.transpose(0, 1, 3, 2, 4).reshape(
        n_pairs, N_HEADS_B // 2, GMLP_BLOCK, 2 * GMLP_BLOCK).astype(jnp.bfloat16)
    b_s_full = jnp.repeat(b_s.transpose(0, 2, 1), HEAD_DIM_B, axis=2)
    table = jnp.concatenate([pad_rows(conv_a_w, _CONV_D_ROW),
                             pad_rows(conv_d_w, _B_S_ROW - _CONV_D_ROW), b_s_full], axis=1)
    by_pair = lambda v: v.reshape(n_pairs, 2, -1)
    ln_g2, ln_b2, b_gate2 = by_pair(ln_g), by_pair(ln_b), by_pair(b_ple_gate)
    vectors = {
        "b_in_e": b_in_e * even_scale, "conv_a_b": conv_a_b, "ln_a_g": 0.5 * ln_a_g,
        "ln_a_b": 0.5 * ln_a_b, "ln_v_g": ln_v_g, "ln_v_b": ln_v_b, "b_out_e": inv_alpha * b_out_e,
        "ln_g_e": ln_g2[:, 0], "ln_b_e": ln_b2[:, 0], "b_gate_e": 0.5 * b_gate2[:, 0],
        "b_in_o": b_in_o * odd_scale, "pool_scale": pool_scale, "b_out_o": inv_alpha * b_out_o,
        "ln_g_o": ln_g2[:, 1], "ln_b_o": ln_b2[:, 1], "b_gate_o": 0.5 * b_gate2[:, 1],
    }
    vec = jnp.concatenate([vectors[name] for name, _ in _VECTOR_FIELDS], axis=1)[:, None, :]

    for j in range(n_pairs):
        e, o = 2 * j, 2 * j + 1
        x = _run_pair(j, x, p, (
            (w_in_e_s, j), (w_out_e_s, j), (w_gate_s, e), (w_ple_s, e),
            (w_in_o_s, j), (w_out_o_s, j), (w_gate_s, o), (w_ple_s, o),
            (w_pairs, j), (w_pool.astype(jnp.bfloat16), j), (table, j), (vec, j)))
    return x
```

```python
import functools

import jax
import jax.numpy as jnp
from jax import lax
from jax.experimental import pallas as pl
from jax.experimental.pallas import tpu as pltpu

D_MODEL = 1024
DEPTH = 4
D_PLE = 256
W_BRANCH = D_MODEL // 2
N_IN_BLOCKS = 6
CONV_A_WIDTH = 31
GMLP_BLOCK = 128
N_HEADS_B = 8
HEAD_DIM_B = W_BRANCH // N_HEADS_B
POOL_WINDOWS = (2, 4, 8, 16)
GROUP_DIM_C = W_BRANCH // len(POOL_WINDOWS)
CONV_D_WIDTH = 3
DEEPNORM_ALPHA = (2.0 * DEPTH) ** 0.25
LN_EPS = 1e-5

LANES = 128
SUBLANES = 8
SEQ_TILE = 512
HALO = 32
VMEM_LIMIT_BYTES = 56 * 1024 * 1024

N_RESIDENT_INPUTS = 12

_VECTOR_FIELDS = (
    ("b_in_e", N_IN_BLOCKS * W_BRANCH), ("conv_a_b", W_BRANCH), ("ln_a_g", W_BRANCH),
    ("ln_a_b", W_BRANCH), ("ln_v_g", W_BRANCH), ("ln_v_b", W_BRANCH), ("b_out_e", D_MODEL),
    ("ln_g_e", D_MODEL), ("ln_b_e", D_MODEL), ("b_gate_e", D_MODEL),
    ("b_in_o", N_IN_BLOCKS * W_BRANCH), ("pool_scale", W_BRANCH), ("b_out_o", D_MODEL),
    ("ln_g_o", D_MODEL), ("ln_b_o", D_MODEL), ("b_gate_o", D_MODEL))
_CONV_D_ROW = 32
_B_S_ROW = 40

_GELU_C1 = 0.7978845608028654
_GELU_C2 = 0.044715


def _vector_views(vec_ref):
    views, off = {}, 0
    for name, width in _VECTOR_FIELDS:
        views[name] = vec_ref.at[:, off:off + width]
        off += width
    return views


def _slab_kernel(w_ref, scale_ref, o_ref):
    o_ref[...] = (w_ref[...] * scale_ref[...]).astype(o_ref.dtype)


def _to_slabs(w, col_scale):
    n_layers, k, n = w.shape
    n_slabs = n // W_BRANCH
    return pl.pallas_call(
        _slab_kernel,
        grid=(n_layers, n_slabs),
        in_specs=[pl.BlockSpec((None, k, W_BRANCH), lambda l, s: (l, 0, s)),
                  pl.BlockSpec((1, W_BRANCH), lambda l, s: (0, s))],
        out_specs=pl.BlockSpec((None, None, k, W_BRANCH), lambda l, s: (l, s, 0, 0)),
        out_shape=jax.ShapeDtypeStruct((n_layers, n_slabs, k, W_BRANCH), jnp.bfloat16),
        compiler_params=pltpu.CompilerParams(dimension_semantics=("arbitrary", "arbitrary")),
        name="weight_slabs",
    )(w, col_scale)


def _silu_of_half(h):
    return h * jnp.tanh(h) + h


def _times_sigmoid_of_half(half_v, h):
    return half_v * jnp.tanh(h) + half_v


def _gelu(x):
    inner = x * (_GELU_C1 + (_GELU_C1 * _GELU_C2) * (x * x))
    h = 0.5 * x
    return h * jnp.tanh(inner) + h


def _layer_norm(x, g, b, eps=LN_EPS):
    mu = jnp.mean(x, axis=-1, keepdims=True)
    d = x - mu
    var = jnp.mean(d * d, axis=-1, keepdims=True)
    return d * lax.rsqrt(var + eps) * g + b


def _proj(xb, w_ref, b_ref, col):
    sl = slice(col * W_BRANCH, (col + 1) * W_BRANCH)
    return jnp.dot(xb, w_ref[col], preferred_element_type=jnp.float32) + b_ref[:, sl]


def _dot_slabs(lhs, w_ref):
    return jnp.concatenate(
        [jnp.dot(lhs, w_ref[i], preferred_element_type=jnp.float32) for i in range(w_ref.shape[0])],
        axis=-1)


def _half_ple(p, w_ple_ref):
    return _dot_slabs(p.astype(jnp.bfloat16), w_ple_ref)


def _deepnorm(x, mix_over_alpha, ln_g_ref, ln_b_ref):
    return _layer_norm(x + mix_over_alpha, ln_g_ref[...], ln_b_ref[...], LN_EPS / DEEPNORM_ALPHA ** 2)


def _gated_embed(h, half_ple, w_gate_ref, b_gate_ref):
    half_z = _dot_slabs(h.astype(jnp.bfloat16), w_gate_ref) + b_gate_ref[...]
    return h + _times_sigmoid_of_half(half_ple, half_z)


def _causal_conv_a(a_ext_ref, conv_w_ref, conv_b_ref, tile):
    a_ext = a_ext_ref[...]
    delayed = [a_ext] + [pltpu.roll(a_ext, j, axis=0) for j in range(1, 4)]
    partial = []
    for e in range(2):
        lo = HALO - SUBLANES * e
        acc = None
        for i in range(4):
            for j in range(4):
                s = 8 * i + 4 * e + j
                if s >= CONV_A_WIDTH:
                    continue
                k = CONV_A_WIDTH - 1 - s
                term = conv_w_ref[k:k + 1, :] * delayed[j][lo - 8 * i:HALO + tile - 8 * i, :]
                acc = term if acc is None else acc + term
        partial.append(acc)
    shifted = pltpu.roll(partial[1], 4, axis=0)[SUBLANES:, :]
    return partial[0] + shifted + conv_b_ref[...]


def _even_half(x, get_half_ple, out, w_in_ref, b_in_ref, conv_w_ref, conv_b_ref, ln_a_g_ref,
               ln_a_b_ref, ln_v_g_ref, ln_v_b_ref, w_s_ref, b_s_ref, w_out_ref, b_out_ref,
               ln_g_ref, ln_b_ref, w_gate_ref, b_gate_ref, a_ext_ref):
    tile = x.shape[0]
    xb = x.astype(jnp.bfloat16)
    yield

    a_ext_ref[HALO:HALO + tile, :] = _times_sigmoid_of_half(
        _proj(xb, w_in_ref, b_in_ref, 0), _proj(xb, w_in_ref, b_in_ref, 1))
    conv = _causal_conv_a(a_ext_ref, conv_w_ref, conv_b_ref, tile)
    a_ext_ref[0:HALO, :] = a_ext_ref[tile:tile + HALO, :]
    a = _silu_of_half(_layer_norm(conv, ln_a_g_ref[...], ln_a_b_ref[...]))
    yield

    a = a * _silu_of_half(_proj(xb, w_in_ref, b_in_ref, 2))
    u = _gelu(_proj(xb, w_in_ref, b_in_ref, 3))
    v = _layer_norm(_gelu(_proj(xb, w_in_ref, b_in_ref, 4)), ln_v_g_ref[...], ln_v_b_ref[...])
    g_gate = _silu_of_half(_proj(xb, w_in_ref, b_in_ref, 5))
    vb = v.astype(jnp.bfloat16)
    n_blocks = tile // GMLP_BLOCK
    row = lax.broadcasted_iota(jnp.int32, (GMLP_BLOCK, 2 * GMLP_BLOCK), 0)
    col = lax.broadcasted_iota(jnp.int32, (GMLP_BLOCK, 2 * GMLP_BLOCK), 1)
    tril = (col & (GMLP_BLOCK - 1)) <= row
    lane = lax.broadcasted_iota(jnp.int32, (GMLP_BLOCK, LANES), 1)
    first_head = lane < HEAD_DIM_B
    zero = jnp.zeros((GMLP_BLOCK, LANES), jnp.bfloat16)
    yield

    sg_cols = []
    for j in range(W_BRANCH // LANES):
        w_pair = jnp.where(tril, w_s_ref[j], jnp.zeros_like(w_s_ref[j]))
        rhs = []
        for n in range(n_blocks):
            vt = vb[n * GMLP_BLOCK:(n + 1) * GMLP_BLOCK, j * LANES:(j + 1) * LANES]
            rhs.append(jnp.concatenate(
                [jnp.where(first_head, vt, zero), jnp.where(first_head, zero, vt)], axis=0))
        rhs = jnp.concatenate(rhs, axis=1)
        mixed = jnp.dot(w_pair, rhs, preferred_element_type=jnp.float32)
        sg_cols.append(jnp.concatenate(
            [mixed[:, n * LANES:(n + 1) * LANES] for n in range(n_blocks)], axis=0))
    sg = jnp.concatenate(sg_cols, axis=1)
    b_s = jnp.concatenate([b_s_ref[...]] * n_blocks, axis=0)
    g = u * (sg + b_s) * g_gate
    y = jnp.concatenate([a, g], axis=-1).astype(jnp.bfloat16)
    yield

    h = _deepnorm(x, _dot_slabs(y, w_out_ref) + b_out_ref[...], ln_g_ref, ln_b_ref)
    yield

    out(_gated_embed(h, get_half_ple(), w_gate_ref, b_gate_ref))


def _odd_half(x, get_half_ple, out, seq_tile_idx, w_in_ref, b_in_ref, w_pool_ref, pool_scale_ref,
              conv_w_ref, w_out_ref, b_out_ref, ln_g_ref, ln_b_ref, w_gate_ref, b_gate_ref,
              c_ext_ref, d_ext_ref):
    tile = x.shape[0]
    xb = x.astype(jnp.bfloat16)
    yield

    c_val = _proj(xb, w_in_ref, b_in_ref, 0)
    pos = (seq_tile_idx * tile + 1
           + lax.broadcasted_iota(jnp.int32, (tile, GROUP_DIM_C), 0)).astype(jnp.float32)
    c_ext_ref[HALO:HALO + tile, :] = c_val
    pooled = []
    for gi, win in enumerate(POOL_WINDOWS):
        lanes = slice(gi * GROUP_DIM_C, (gi + 1) * GROUP_DIM_C)
        wsum = c_ext_ref[:, lanes]
        span = 1
        while span < win:
            wsum = wsum + pltpu.roll(wsum, span, axis=0)
            span *= 2
        mean = wsum[HALO:, :] / jnp.minimum(pos, float(win))
        pooled.append((mean - c_val[:, lanes]).astype(jnp.bfloat16))
    c_ext_ref[0:HALO, :] = c_ext_ref[tile:tile + HALO, :]
    d_in = _proj(xb, w_in_ref, b_in_ref, 4) * _proj(xb, w_in_ref, b_in_ref, 2)
    d_ext_ref[HALO:HALO + tile, :] = d_in
    d_ext = d_ext_ref[...]
    conv = conv_w_ref[CONV_D_WIDTH - 1:CONV_D_WIDTH, :] * d_in
    for s in range(1, CONV_D_WIDTH):
        k = CONV_D_WIDTH - 1 - s
        conv = conv + conv_w_ref[k:k + 1, :] * pltpu.roll(d_ext, s, axis=0)[HALO:, :]
    d_ext_ref[0:HALO, :] = d_ext_ref[tile:tile + HALO, :]
    yield

    d = _proj(xb, w_in_ref, b_in_ref, 3) * conv * _silu_of_half(_proj(xb, w_in_ref, b_in_ref, 5))
    c_gate = _silu_of_half(_proj(xb, w_in_ref, b_in_ref, 1))
    yield

    c = jnp.concatenate(
        [jnp.dot(pooled[gi], w_pool_ref[gi], preferred_element_type=jnp.float32)
         for gi in range(len(POOL_WINDOWS))], axis=-1) * pool_scale_ref[...]
    y = jnp.concatenate([c * c_gate, d], axis=-1).astype(jnp.bfloat16)
    yield

    h = _deepnorm(x, _dot_slabs(y, w_out_ref) + b_out_ref[...], ln_g_ref, ln_b_ref)
    yield

    out(_gated_embed(h, get_half_ple(), w_gate_ref, b_gate_ref))


_MXU_ORDER = "EO" + "EOEOOEOeEoOE"


def _pair_kernel(n_seq_tiles, x_ref, p_even_ref, p_odd_ref, *refs):
    (w_in_e, w_out_e, w_gate_e, w_ple_e, w_in_o, w_out_o, w_gate_o, w_ple_o,
     w_s_ref, w_pool_ref, table_ref, vec_ref) = refs[:N_RESIDENT_INPUTS]
    o_ref, x_mid_ref, a_ext_ref, c_ext_ref, d_ext_ref = refs[N_RESIDENT_INPUTS:]
    vec = _vector_views(vec_ref)
    even = (w_in_e, vec["b_in_e"], table_ref.at[0:_CONV_D_ROW], vec["conv_a_b"],
            vec["ln_a_g"], vec["ln_a_b"], vec["ln_v_g"], vec["ln_v_b"], w_s_ref,
            table_ref.at[_B_S_ROW:_B_S_ROW + GMLP_BLOCK], w_out_e, vec["b_out_e"],
            vec["ln_g_e"], vec["ln_b_e"], w_gate_e, vec["b_gate_e"])
    odd = (w_in_o, vec["b_in_o"], w_pool_ref, vec["pool_scale"],
           table_ref.at[_CONV_D_ROW:_B_S_ROW], w_out_o, vec["b_out_o"],
           vec["ln_g_o"], vec["ln_b_o"], w_gate_o, vec["b_gate_o"])
    step = pl.program_id(0)
    even_seq_idx = lax.rem(step, n_seq_tiles)
    odd_seq_idx = lax.rem(jnp.maximum(step - 1, 0), n_seq_tiles)
    slot = lax.rem(step, 2)
    halo_zeros = jnp.zeros((HALO, W_BRANCH), jnp.float32)

    @pl.when(step == 0)
    def _():
        def zero_rows(r, carry):
            x_mid_ref[1, pl.ds(pl.multiple_of(r * SUBLANES, SUBLANES), SUBLANES), :] = (
                jnp.zeros((SUBLANES, D_MODEL), jnp.float32))
            return carry
        lax.fori_loop(0, x_mid_ref.shape[1] // SUBLANES, zero_rows, 0)

    @pl.when(even_seq_idx == 0)
    def _():
        a_ext_ref[0:HALO, :] = halo_zeros

    @pl.when(odd_seq_idx == 0)
    def _():
        c_ext_ref[0:HALO, :] = halo_zeros
        d_ext_ref[0:HALO, :] = halo_zeros

    def write_mid(v):
        x_mid_ref[slot] = v

    def write_out(v):
        o_ref[...] = v

    half_ple = {}
    halves = {
        "E": _even_half(x_ref[...], lambda: half_ple["e"], write_mid, *even, a_ext_ref),
        "O": _odd_half(x_mid_ref[1 - slot], lambda: half_ple["o"], write_out, odd_seq_idx, *odd,
                       c_ext_ref, d_ext_ref),
    }
    for which in _MXU_ORDER:
        if which == "e":
            half_ple["e"] = _half_ple(p_even_ref[...], w_ple_e)
        elif which == "o":
            half_ple["o"] = _half_ple(p_odd_ref[...], w_ple_o)
        else:
            next(halves[which], None)
    assert all(next(h, "done") == "done" for h in halves.values()), "_MXU_ORDER leaves work unemitted"


def _layer_spec(arr, index):
    zeros = (0,) * (arr.ndim - 1)
    return pl.BlockSpec((None,) + arr.shape[1:], lambda i: (index,) + zeros,
                        pipeline_mode=pl.Buffered(1))


def _run_pair(pair, x, p, resident):
    batch, seq, _ = x.shape
    assert len(resident) == N_RESIDENT_INPUTS
    layer = 2 * pair
    n_seq_tiles = seq // SEQ_TILE
    n_tiles = batch * n_seq_tiles

    def even_tile(i):
        t = jnp.minimum(i, n_tiles - 1)
        return t // n_seq_tiles, t % n_seq_tiles

    def odd_tile(i):
        t = jnp.maximum(i - 1, 0)
        return t // n_seq_tiles, t % n_seq_tiles

    x_block = (None, SEQ_TILE, D_MODEL)
    p_block = (None, None, SEQ_TILE, D_PLE)
    return pl.pallas_call(
        functools.partial(_pair_kernel, n_seq_tiles),
        grid=(n_tiles + 1,),
        in_specs=[pl.BlockSpec(x_block, lambda i: (*even_tile(i), 0)),
                  pl.BlockSpec(p_block, lambda i: (layer, *even_tile(i), 0)),
                  pl.BlockSpec(p_block, lambda i: (layer + 1, *odd_tile(i), 0))]
                 + [_layer_spec(arr, index) for arr, index in resident],
        out_specs=pl.BlockSpec(x_block, lambda i: (*odd_tile(i), 0)),
        out_shape=jax.ShapeDtypeStruct(x.shape, x.dtype),
        scratch_shapes=[pltpu.VMEM((2, SEQ_TILE, D_MODEL), jnp.float32)]
                       + [pltpu.VMEM((HALO + SEQ_TILE, W_BRANCH), jnp.float32)] * 3,
        compiler_params=pltpu.CompilerParams(
            dimension_semantics=("arbitrary",),
            vmem_limit_bytes=VMEM_LIMIT_BYTES),
        name=f"layers{layer}{layer + 1}",
    )(x, p, p, *[arr for arr, _ in resident])


def kernel(x, p, w_in_e, b_in_e, conv_a_w, conv_a_b, ln_a_g, ln_a_b, ln_v_g, ln_v_b, w_s, b_s,
           w_out_e, b_out_e, w_in_o, b_in_o, w_pool, pool_scale, conv_d_w, w_out_o, b_out_o,
           ln_g, ln_b, w_ple, w_ple_gate, b_ple_gate):
    assert x.shape[1] % SEQ_TILE == 0 and SEQ_TILE % GMLP_BLOCK == 0
    n_pairs = DEPTH // 2
    f32 = jnp.float32

    def block_scale(scales):
        return jnp.repeat(jnp.asarray(scales, f32), W_BRANCH)[None, :]

    def pad_rows(a, rows):
        return jnp.pad(a, ((0, 0), (0, rows - a.shape[1]), (0, 0)))

    even_scale = block_scale((0.5, 0.5, 0.5, 1.0, 1.0, 0.5))
    odd_scale = block_scale((1.0, 0.5, 1.0, 1.0, 1.0, 0.5))
    inv_alpha = 1.0 / DEEPNORM_ALPHA
    half_row = jnp.full((1, D_MODEL), 0.5, f32)
    inv_alpha_row = jnp.full((1, D_MODEL), inv_alpha, f32)

    w_in_e_s = _to_slabs(w_in_e, even_scale)
    w_in_o_s = _to_slabs(w_in_o, odd_scale)
    w_out_e_s = _to_slabs(w_out_e, inv_alpha_row)
    w_out_o_s = _to_slabs(w_out_o, inv_alpha_row)
    w_gate_s = _to_slabs(w_ple_gate, half_row)
    w_ple_s = _to_slabs(w_ple, half_row)

    w_pairs = w_s.reshape(n_pairs, N_HEADS_B // 2, 2, GMLP_BLOCK, GMLP_BLOCK)
    w_pairs = w_pairs.transpose(0, 1, 3, 2, 4).reshape(
        n_pairs, N_HEADS_B // 2, GMLP_BLOCK, 2 * GMLP_BLOCK).astype(jnp.bfloat16)
    b_s_full = jnp.repeat(b_s.transpose(0, 2, 1), HEAD_DIM_B, axis=2)
    table = jnp.concatenate([pad_rows(conv_a_w, _CONV_D_ROW),
                             pad_rows(conv_d_w, _B_S_ROW - _CONV_D_ROW), b_s_full], axis=1)
    by_pair = lambda v: v.reshape(n_pairs, 2, -1)
    ln_g2, ln_b2, b_gate2 = by_pair(ln_g), by_pair(ln_b), by_pair(b_ple_gate)
    vectors = {
        "b_in_e": b_in_e * even_scale, "conv_a_b": conv_a_b, "ln_a_g": 0.5 * ln_a_g,
        "ln_a_b": 0.5 * ln_a_b, "ln_v_g": ln_v_g, "ln_v_b": ln_v_b, "b_out_e": inv_alpha * b_out_e,
        "ln_g_e": ln_g2[:, 0], "ln_b_e": ln_b2[:, 0], "b_gate_e": 0.5 * b_gate2[:, 0],
        "b_in_o": b_in_o * odd_scale, "pool_scale": pool_scale, "b_out_o": inv_alpha * b_out_o,
        "ln_g_o": ln_g2[:, 1], "ln_b_o": ln_b2[:, 1], "b_gate_o": 0.5 * b_gate2[:, 1],
    }
    vec = jnp.concatenate([vectors[name] for name, _ in _VECTOR_FIELDS], axis=1)[:, None, :]

    for j in range(n_pairs):
        e, o = 2 * j, 2 * j + 1
        x = _run_pair(j, x, p, (
            (w_in_e_s, j), (w_out_e_s, j), (w_gate_s, e), (w_ple_s, e),
            (w_in_o_s, j), (w_out_o_s, j), (w_gate_s, o), (w_ple_s, o),
            (w_pairs, j), (w_pool.astype(jnp.bfloat16), j), (table, j), (vec, j)))
    return x
```

```python
import functools

import jax
import jax.numpy as jnp
from jax import lax
from jax.experimental import pallas as pl
from jax.experimental.pallas import tpu as pltpu

D_MODEL = 1024
DEPTH = 4
D_PLE = 256
W_BRANCH = D_MODEL // 2
N_IN_BLOCKS = 6
CONV_A_WIDTH = 31
GMLP_BLOCK = 128
N_HEADS_B = 8
HEAD_DIM_B = W_BRANCH // N_HEADS_B
POOL_WINDOWS = (2, 4, 8, 16)
GROUP_DIM_C = W_BRANCH // len(POOL_WINDOWS)
CONV_D_WIDTH = 3
DEEPNORM_ALPHA = (2.0 * DEPTH) ** 0.25
LN_EPS = 1e-5

LANES = 128
SUBLANES = 8
SEQ_TILE = 512
HALO = 32
VMEM_LIMIT_BYTES = 56 * 1024 * 1024

N_RESIDENT_INPUTS = 12

_VECTOR_FIELDS = (
    ("b_in_e", N_IN_BLOCKS * W_BRANCH), ("conv_a_b", W_BRANCH), ("ln_a_g", W_BRANCH),
    ("ln_a_b", W_BRANCH), ("ln_v_g", W_BRANCH), ("ln_v_b", W_BRANCH), ("b_out_e", D_MODEL),
    ("ln_g_e", D_MODEL), ("ln_b_e", D_MODEL), ("b_gate_e", D_MODEL),
    ("b_in_o", N_IN_BLOCKS * W_BRANCH), ("pool_scale", W_BRANCH), ("b_out_o", D_MODEL),
    ("ln_g_o", D_MODEL), ("ln_b_o", D_MODEL), ("b_gate_o", D_MODEL))
_CONV_D_ROW = 32
_B_S_ROW = 40

_GELU_C1 = 0.7978845608028654
_GELU_C2 = 0.044715


def _vector_views(vec_ref):
    views, off = {}, 0
    for name, width in _VECTOR_FIELDS:
        views[name] = vec_ref.at[:, off:off + width]
        off += width
    return views


def _slab_kernel(w_ref, scale_ref, o_ref):
    o_ref[...] = (w_ref[...] * scale_ref[...]).astype(o_ref.dtype)


def _to_slabs(w, col_scale):
    n_layers, k, n = w.shape
    n_slabs = n // W_BRANCH
    return pl.pallas_call(
        _slab_kernel,
        grid=(n_layers, n_slabs),
        in_specs=[pl.BlockSpec((None, k, W_BRANCH), lambda l, s: (l, 0, s)),
                  pl.BlockSpec((1, W_BRANCH), lambda l, s: (0, s))],
        out_specs=pl.BlockSpec((None, None, k, W_BRANCH), lambda l, s: (l, s, 0, 0)),
        out_shape=jax.ShapeDtypeStruct((n_layers, n_slabs, k, W_BRANCH), jnp.bfloat16),
        compiler_params=pltpu.CompilerParams(dimension_semantics=("arbitrary", "arbitrary")),
        name="weight_slabs",
    )(w, col_scale)


def _silu_of_half(h):
    return h * jnp.tanh(h) + h


def _times_sigmoid_of_half(half_v, h):
    return half_v * jnp.tanh(h) + half_v


def _gelu(x):
    inner = x * (_GELU_C1 + (_GELU_C1 * _GELU_C2) * (x * x))
    h = 0.5 * x
    return h * jnp.tanh(inner) + h


def _layer_norm(x, g, b, eps=LN_EPS):
    mu = jnp.mean(x, axis=-1, keepdims=True)
    d = x - mu
    var = jnp.mean(d * d, axis=-1, keepdims=True)
    return d * lax.rsqrt(var + eps) * g + b


def _proj(xb, w_ref, b_ref, col):
    sl = slice(col * W_BRANCH, (col + 1) * W_BRANCH)
    return jnp.dot(xb, w_ref[col], preferred_element_type=jnp.float32) + b_ref[:, sl]


def _dot_slabs(lhs, w_ref):
    return jnp.concatenate(
        [jnp.dot(lhs, w_ref[i], preferred_element_type=jnp.float32) for i in range(w_ref.shape[0])],
        axis=-1)


def _half_ple(p, w_ple_ref):
    return _dot_slabs(p.astype(jnp.bfloat16), w_ple_ref)


def _deepnorm(x, mix_over_alpha, ln_g_ref, ln_b_ref):
    return _layer_norm(x + mix_over_alpha, ln_g_ref[...], ln_b_ref[...], LN_EPS / DEEPNORM_ALPHA ** 2)


def _gated_embed(h, half_ple, w_gate_ref, b_gate_ref):
    half_z = _dot_slabs(h.astype(jnp.bfloat16), w_gate_ref) + b_gate_ref[...]
    return h + _times_sigmoid_of_half(half_ple, half_z)


def _causal_conv_a(a_ext_ref, conv_w_ref, conv_b_ref, tile):
    a_ext = a_ext_ref[...]
    partial = [None, None]
    for j in range(4):
        delayed = a_ext if j == 0 else pltpu.roll(a_ext, j, axis=0)
        for e in range(2):
            lo = HALO - SUBLANES * e
            for i in range(4):
                s = 8 * i + 4 * e + j
                if s >= CONV_A_WIDTH:
                    continue
                k = CONV_A_WIDTH - 1 - s
                term = conv_w_ref[k:k + 1, :] * delayed[lo - 8 * i:HALO + tile - 8 * i, :]
                partial[e] = term if partial[e] is None else partial[e] + term
    shifted = pltpu.roll(partial[1], 4, axis=0)[SUBLANES:, :]
    return partial[0] + shifted + conv_b_ref[...]


def _even_half(x, get_half_ple, out, w_in_ref, b_in_ref, conv_w_ref, conv_b_ref, ln_a_g_ref,
               ln_a_b_ref, ln_v_g_ref, ln_v_b_ref, w_s_ref, b_s_ref, w_out_ref, b_out_ref,
               ln_g_ref, ln_b_ref, w_gate_ref, b_gate_ref, a_ext_ref):
    tile = x.shape[0]
    xb = x.astype(jnp.bfloat16)
    yield

    a_ext_ref[HALO:HALO + tile, :] = _times_sigmoid_of_half(
        _proj(xb, w_in_ref, b_in_ref, 0), _proj(xb, w_in_ref, b_in_ref, 1))
    conv = _causal_conv_a(a_ext_ref, conv_w_ref, conv_b_ref, tile)
    a_ext_ref[0:HALO, :] = a_ext_ref[tile:tile + HALO, :]
    a = _silu_of_half(_layer_norm(conv, ln_a_g_ref[...], ln_a_b_ref[...]))
    yield

    a = a * _silu_of_half(_proj(xb, w_in_ref, b_in_ref, 2))
    u = _gelu(_proj(xb, w_in_ref, b_in_ref, 3))
    v = _layer_norm(_gelu(_proj(xb, w_in_ref, b_in_ref, 4)), ln_v_g_ref[...], ln_v_b_ref[...])
    g_gate = _silu_of_half(_proj(xb, w_in_ref, b_in_ref, 5))
    vb = v.astype(jnp.bfloat16)
    n_blocks = tile // GMLP_BLOCK
    row = lax.broadcasted_iota(jnp.int32, (GMLP_BLOCK, 2 * GMLP_BLOCK), 0)
    col = lax.broadcasted_iota(jnp.int32, (GMLP_BLOCK, 2 * GMLP_BLOCK), 1)
    tril = (col & (GMLP_BLOCK - 1)) <= row
    lane = lax.broadcasted_iota(jnp.int32, (GMLP_BLOCK, LANES), 1)
    first_head = lane < HEAD_DIM_B
    zero = jnp.zeros((GMLP_BLOCK, LANES), jnp.bfloat16)
    yield

    sg_cols = []
    for j in range(W_BRANCH // LANES):
        w_pair = jnp.where(tril, w_s_ref[j], jnp.zeros_like(w_s_ref[j]))
        rhs = []
        for n in range(n_blocks):
            vt = vb[n * GMLP_BLOCK:(n + 1) * GMLP_BLOCK, j * LANES:(j + 1) * LANES]
            rhs.append(jnp.concatenate(
                [jnp.where(first_head, vt, zero), jnp.where(first_head, zero, vt)], axis=0))
        rhs = jnp.concatenate(rhs, axis=1)
        mixed = jnp.dot(w_pair, rhs, preferred_element_type=jnp.float32)
        sg_cols.append(jnp.concatenate(
            [mixed[:, n * LANES:(n + 1) * LANES] for n in range(n_blocks)], axis=0))
    sg = jnp.concatenate(sg_cols, axis=1)
    b_s = jnp.concatenate([b_s_ref[...]] * n_blocks, axis=0)
    g = u * (sg + b_s) * g_gate
    y = jnp.concatenate([a, g], axis=-1).astype(jnp.bfloat16)
    yield

    h = _deepnorm(x, _dot_slabs(y, w_out_ref) + b_out_ref[...], ln_g_ref, ln_b_ref)
    yield

    out(_gated_embed(h, get_half_ple(), w_gate_ref, b_gate_ref))


def _odd_half(x, get_half_ple, out, seq_tile_idx, w_in_ref, b_in_ref, w_pool_ref, pool_scale_ref,
              conv_w_ref, w_out_ref, b_out_ref, ln_g_ref, ln_b_ref, w_gate_ref, b_gate_ref,
              c_ext_ref, d_ext_ref):
    tile = x.shape[0]
    xb = x.astype(jnp.bfloat16)
    yield

    c_val = _proj(xb, w_in_ref, b_in_ref, 0)
    pos = (seq_tile_idx * tile + 1
           + lax.broadcasted_iota(jnp.int32, (tile, GROUP_DIM_C), 0)).astype(jnp.float32)
    c_ext_ref[HALO:HALO + tile, :] = c_val
    pooled = []
    for gi, win in enumerate(POOL_WINDOWS):
        lanes = slice(gi * GROUP_DIM_C, (gi + 1) * GROUP_DIM_C)
        wsum = c_ext_ref[:, lanes]
        span = 1
        while span < win:
            wsum = wsum + pltpu.roll(wsum, span, axis=0)
            span *= 2
        mean = wsum[HALO:, :] / jnp.minimum(pos, float(win))
        pooled.append((mean - c_val[:, lanes]).astype(jnp.bfloat16))
    c_ext_ref[0:HALO, :] = c_ext_ref[tile:tile + HALO, :]
    d_in = _proj(xb, w_in_ref, b_in_ref, 4) * _proj(xb, w_in_ref, b_in_ref, 2)
    d_ext_ref[HALO:HALO + tile, :] = d_in
    d_ext = d_ext_ref[...]
    conv = conv_w_ref[CONV_D_WIDTH - 1:CONV_D_WIDTH, :] * d_in
    for s in range(1, CONV_D_WIDTH):
        k = CONV_D_WIDTH - 1 - s
        conv = conv + conv_w_ref[k:k + 1, :] * pltpu.roll(d_ext, s, axis=0)[HALO:, :]
    d_ext_ref[0:HALO, :] = d_ext_ref[tile:tile + HALO, :]
    yield

    d = _proj(xb, w_in_ref, b_in_ref, 3) * conv * _silu_of_half(_proj(xb, w_in_ref, b_in_ref, 5))
    c_gate = _silu_of_half(_proj(xb, w_in_ref, b_in_ref, 1))
    yield

    c = jnp.concatenate(
        [jnp.dot(pooled[gi], w_pool_ref[gi], preferred_element_type=jnp.float32)
         for gi in range(len(POOL_WINDOWS))], axis=-1) * pool_scale_ref[...]
    y = jnp.concatenate([c * c_gate, d], axis=-1).astype(jnp.bfloat16)
    yield

    h = _deepnorm(x, _dot_slabs(y, w_out_ref) + b_out_ref[...], ln_g_ref, ln_b_ref)
    yield

    out(_gated_embed(h, get_half_ple(), w_gate_ref, b_gate_ref))


_MXU_ORDER = "EO" + "EOEOOEOeEoOE"


def _pair_kernel(n_seq_tiles, x_ref, p_even_ref, p_odd_ref, *refs):
    (w_in_e, w_out_e, w_gate_e, w_ple_e, w_in_o, w_out_o, w_gate_o, w_ple_o,
     w_s_ref, w_pool_ref, table_ref, vec_ref) = refs[:N_RESIDENT_INPUTS]
    o_ref, x_mid_ref, a_ext_ref, c_ext_ref, d_ext_ref = refs[N_RESIDENT_INPUTS:]
    vec = _vector_views(vec_ref)
    even = (w_in_e, vec["b_in_e"], table_ref.at[0:_CONV_D_ROW], vec["conv_a_b"],
            vec["ln_a_g"], vec["ln_a_b"], vec["ln_v_g"], vec["ln_v_b"], w_s_ref,
            table_ref.at[_B_S_ROW:_B_S_ROW + GMLP_BLOCK], w_out_e, vec["b_out_e"],
            vec["ln_g_e"], vec["ln_b_e"], w_gate_e, vec["b_gate_e"])
    odd = (w_in_o, vec["b_in_o"], w_pool_ref, vec["pool_scale"],
           table_ref.at[_CONV_D_ROW:_B_S_ROW], w_out_o, vec["b_out_o"],
           vec["ln_g_o"], vec["ln_b_o"], w_gate_o, vec["b_gate_o"])
    step = pl.program_id(0)
    even_seq_idx = lax.rem(step, n_seq_tiles)
    odd_seq_idx = lax.rem(jnp.maximum(step - 1, 0), n_seq_tiles)
    slot = lax.rem(step, 2)
    halo_zeros = jnp.zeros((HALO, W_BRANCH), jnp.float32)

    @pl.when(step == 0)
    def _():
        def zero_rows(r, carry):
            x_mid_ref[1, pl.ds(pl.multiple_of(r * SUBLANES, SUBLANES), SUBLANES), :] = (
                jnp.zeros((SUBLANES, D_MODEL), jnp.float32))
            return carry
        lax.fori_loop(0, x_mid_ref.shape[1] // SUBLANES, zero_rows, 0)

    @pl.when(even_seq_idx == 0)
    def _():
        a_ext_ref[0:HALO, :] = halo_zeros

    @pl.when(odd_seq_idx == 0)
    def _():
        c_ext_ref[0:HALO, :] = halo_zeros
        d_ext_ref[0:HALO, :] = halo_zeros

    def write_mid(v):
        x_mid_ref[slot] = v

    def write_out(v):
        o_ref[...] = v

    half_ple = {}
    halves = {
        "E": _even_half(x_ref[...], lambda: half_ple["e"], write_mid, *even, a_ext_ref),
        "O": _odd_half(x_mid_ref[1 - slot], lambda: half_ple["o"], write_out, odd_seq_idx, *odd,
                       c_ext_ref, d_ext_ref),
    }
    for which in _MXU_ORDER:
        if which == "e":
            half_ple["e"] = _half_ple(p_even_ref[...], w_ple_e)
        elif which == "o":
            half_ple["o"] = _half_ple(p_odd_ref[...], w_ple_o)
        else:
            next(halves[which], None)
    assert all(next(h, "done") == "done" for h in halves.values()), "_MXU_ORDER leaves work unemitted"


def _layer_spec(arr, index):
    zeros = (0,) * (arr.ndim - 1)
    return pl.BlockSpec((None,) + arr.shape[1:], lambda i: (index,) + zeros,
                        pipeline_mode=pl.Buffered(1))


def _run_pair(pair, x, p, resident):
    batch, seq, _ = x.shape
    assert len(resident) == N_RESIDENT_INPUTS
    layer = 2 * pair
    n_seq_tiles = seq // SEQ_TILE
    n_tiles = batch * n_seq_tiles

    def even_tile(i):
        t = jnp.minimum(i, n_tiles - 1)
        return t // n_seq_tiles, t % n_seq_tiles

    def odd_tile(i):
        t = jnp.maximum(i - 1, 0)
        return t // n_seq_tiles, t % n_seq_tiles

    x_block = (None, SEQ_TILE, D_MODEL)
    p_block = (None, None, SEQ_TILE, D_PLE)
    return pl.pallas_call(
        functools.partial(_pair_kernel, n_seq_tiles),
        grid=(n_tiles + 1,),
        in_specs=[pl.BlockSpec(x_block, lambda i: (*even_tile(i), 0)),
                  pl.BlockSpec(p_block, lambda i: (layer, *even_tile(i), 0)),
                  pl.BlockSpec(p_block, lambda i: (layer + 1, *odd_tile(i), 0))]
                 + [_layer_spec(arr, index) for arr, index in resident],
        out_specs=pl.BlockSpec(x_block, lambda i: (*odd_tile(i), 0)),
        out_shape=jax.ShapeDtypeStruct(x.shape, x.dtype),
        scratch_shapes=[pltpu.VMEM((2, SEQ_TILE, D_MODEL), jnp.float32)]
                       + [pltpu.VMEM((HALO + SEQ_TILE, W_BRANCH), jnp.float32)] * 3,
        compiler_params=pltpu.CompilerParams(
            dimension_semantics=("arbitrary",),
            vmem_limit_bytes=VMEM_LIMIT_BYTES),
        name=f"layers{layer}{layer + 1}",
    )(x, p, p, *[arr for arr, _ in resident])


def kernel(x, p, w_in_e, b_in_e, conv_a_w, conv_a_b, ln_a_g, ln_a_b, ln_v_g, ln_v_b, w_s, b_s,
           w_out_e, b_out_e, w_in_o, b_in_o, w_pool, pool_scale, conv_d_w, w_out_o, b_out_o,
           ln_g, ln_b, w_ple, w_ple_gate, b_ple_gate):
    assert x.shape[1] % SEQ_TILE == 0 and SEQ_TILE % GMLP_BLOCK == 0
    n_pairs = DEPTH // 2
    f32 = jnp.float32

    def block_scale(scales):
        return jnp.repeat(jnp.asarray(scales, f32), W_BRANCH)[None, :]

    def pad_rows(a, rows):
        return jnp.pad(a, ((0, 0), (0, rows - a.shape[1]), (0, 0)))

    even_scale = block_scale((0.5, 0.5, 0.5, 1.0, 1.0, 0.5))
    odd_scale = block_scale((1.0, 0.5, 1.0, 1.0, 1.0, 0.5))
    inv_alpha = 1.0 / DEEPNORM_ALPHA
    half_row = jnp.full((1, D_MODEL), 0.5, f32)
    inv_alpha_row = jnp.full((1, D_MODEL), inv_alpha, f32)

    w_in_e_s = _to_slabs(w_in_e, even_scale)
    w_in_o_s = _to_slabs(w_in_o, odd_scale)
    w_out_e_s = _to_slabs(w_out_e, inv_alpha_row)
    w_out_o_s = _to_slabs(w_out_o, inv_alpha_row)
    w_gate_s = _to_slabs(w_ple_gate, half_row)
    w_ple_s = _to_slabs(w_ple, half_row)

    w_pairs = w_s.reshape(n_pairs, N_HEADS_B // 2, 2, GMLP_BLOCK, GMLP_BLOCK)
    w_pairs = w_pairs.transpose(0, 1, 3, 2, 4).reshape(
        n_pairs, N_HEADS_B // 2, GMLP_BLOCK, 2 * GMLP_BLOCK).astype(jnp.bfloat16)
    b_s_full = jnp.repeat(b_s.transpose(0, 2, 1), HEAD_DIM_B, axis=2)
    table = jnp.concatenate([pad_rows(conv_a_w, _CONV_D_ROW),
                             pad_rows(conv_d_w, _B_S_ROW - _CONV_D_ROW), b_s_full], axis=1)
    by_pair = lambda v: v.reshape(n_pairs, 2, -1)
    ln_g2, ln_b2, b_gate2 = by_pair(ln_g), by_pair(ln_b), by_pair(b_ple_gate)
    vectors = {
        "b_in_e": b_in_e * even_scale, "conv_a_b": conv_a_b, "ln_a_g": 0.5 * ln_a_g,
        "ln_a_b": 0.5 * ln_a_b, "ln_v_g": ln_v_g, "ln_v_b": ln_v_b, "b_out_e": inv_alpha * b_out_e,
        "ln_g_e": ln_g2[:, 0], "ln_b_e": ln_b2[:, 0], "b_gate_e": 0.5 * b_gate2[:, 0],
        "b_in_o": b_in_o * odd_scale, "pool_scale": pool_scale, "b_out_o": inv_alpha * b_out_o,
        "ln_g_o": ln_g2[:, 1], "ln_b_o": ln_b2[:, 1], "b_gate_o": 0.5 * b_gate2[:, 1],
    }
    vec = jnp.concatenate([vectors[name] for name, _ in _VECTOR_FIELDS], axis=1)[:, None, :]

    for j in range(n_pairs):
        e, o = 2 * j, 2 * j + 1
        x = _run_pair(j, x, p, (
            (w_in_e_s, j), (w_out_e_s, j), (w_gate_s, e), (w_ple_s, e),
            (w_in_o_s, j), (w_out_o_s, j), (w_gate_s, o), (w_ple_s, o),
            (w_pairs, j), (w_pool.astype(jnp.bfloat16), j), (table, j), (vec, j)))
    return x
```
